```python
import jax
import jax.numpy as jnp
from jax import lax
import numpy as np

D_MODEL = 2048
BATCH = 2
SEQ = 8192
DEPTH = 4

GRID_W = 64
CTX_LEN = 256

A_HEADS = 4
A_HD = 192
A_DIM = A_HEADS * A_HD
B_HEADS = 10
B_HD = 64
B_DIM = B_HEADS * B_HD
B_W_LORA = 64
B_A_LORA = 32
B_G_LORA = 128
B_GN_EPS = 64e-5
C_HEADS = 4
C_HDK = 80
C_HDV = 160
C_DK = C_HEADS * C_HDK
C_DV = C_HEADS * C_HDV
C_G_LORA = 16
C_TAU = 16.0

CHUNK = 64
D_FF = 5504
N_EXPERTS = 8
TOP_K = 2
N_DENSE = (DEPTH + 1) // 2
N_MOE = DEPTH // 2
ALPHA = (2.0 * DEPTH) ** 0.25
BETA = (8.0 * DEPTH) ** -0.25
LN_EPS = 1e-6
HEAD_EPS = 1e-5
F32 = jnp.float32

CONV_SIZES = (A_DIM, A_DIM, A_DIM, B_DIM, B_DIM, B_DIM, C_DK, C_DK, C_DV)
CONV_NAMES = ("a_q", "a_k", "a_v", "b_r", "b_k", "b_v", "c_q", "c_k", "c_v")
REST_SIZES = (3 * D_MODEL, A_DIM, 4 * A_HEADS, 2 * B_W_LORA, 2 * B_A_LORA, B_G_LORA, 2 * C_G_LORA, C_DV)
REST_NAMES = ("merge_gates", "a_o", "a_gates", "b_w_code", "b_a_code", "b_g_code", "c_g_code", "c_o")
CONV_W = sum(CONV_SIZES)
N_IN = CONV_W + sum(REST_SIZES)

kernel_name = "hybrid_mlstm_rwkv7_gla_moe_dit"


def _split(a, sizes):
    return jnp.split(a, np.cumsum(sizes)[:-1].tolist(), axis=-1)


def _layernorm(a, eps=LN_EPS):
    af = a.astype(F32)
    mu = af.mean(-1, keepdims=True)
    var = jnp.square(af - mu).mean(-1, keepdims=True)
    return (af - mu) * lax.rsqrt(var + eps)


def _modulate(a, shift, scale):
    return (_layernorm(a) * (1.0 + scale) + shift).astype(a.dtype)


def _post_norm(a, gain_bias):
    return (_layernorm(a) * gain_bias[0] + gain_bias[1]).astype(a.dtype)


def _head_norm(h, eps, rms=False):
    hf = h.astype(F32)
    if not rms:
        hf = hf - hf.mean(-1, keepdims=True)
    return hf * lax.rsqrt(jnp.square(hf).mean(-1, keepdims=True) + eps)


def _heads(a, n):
    return a.astype(F32).reshape(a.shape[:-1] + (n, a.shape[-1] // n))


def _grid_conv(a, conv_w, cols):
    b, t, ch = a.shape
    y = lax.conv_general_dilated(
        a.reshape(b, t // cols, cols, ch), conv_w[:, :, None, :].astype(a.dtype),
        window_strides=(1, 1), padding="SAME",
        dimension_numbers=("NHWC", "HWIO", "NHWC"), feature_group_count=ch)
    return y.reshape(b, t, ch)


def _in_proj(u, w_in, conv_k, cols):
    z = u @ w_in
    zc = _grid_conv(z[..., :CONV_W], conv_k, cols)
    parts = dict(zip(CONV_NAMES, _split(zc, CONV_SIZES)))
    parts.update(zip(REST_NAMES, _split(z[..., CONV_W:], REST_SIZES)))
    return parts


def _to_chunks(a):
    b, t, h = a.shape[:3]
    a = a.reshape((b, t // CHUNK, CHUNK, h) + a.shape[3:])
    return jnp.moveaxis(jnp.moveaxis(a, 1, 0), 3, 2)


def _from_chunks(a):
    nc, b, h, l = a.shape[:4]
    a = jnp.moveaxis(jnp.moveaxis(a, 2, 3), 0, 1)
    return a.reshape((b, nc * l, h) + a.shape[4:])


def _flip_t(xs):
    return tuple(jnp.flip(a, axis=1) for a in xs)


def _two_way(scan_fn, init, ctx_dirs, lat_dirs):
    y_cf, s_f = scan_fn(ctx_dirs[0], init)
    y_lf, _ = scan_fn(lat_dirs[0], s_f)
    y_cb, s_b = scan_fn(_flip_t(ctx_dirs[1]), init)
    y_lb, _ = scan_fn(_flip_t(lat_dirs[1]), s_b)
    return y_cf + jnp.flip(y_cb, 1), y_lf + jnp.flip(y_lb, 1)


def _mlstm_scan(xs, state):
    tril = jnp.tril(jnp.ones((CHUNK, CHUNK), dtype=bool))

    def step(carry, xc):
        c_mat, n_vec, m_prev = carry
        qc, kc, vc, ic, fc = xc
        bcum = jnp.cumsum(fc, axis=-1)
        log_d = jnp.where(tril, bcum[..., :, None] - bcum[..., None, :] + ic[..., None, :], -jnp.inf)
        m_inter = bcum + m_prev[..., None]
        m_t = jnp.maximum(m_inter, log_d.max(-1))
        s = jnp.einsum("bhtd,bhsd->bhts", qc, kc) * jnp.exp(log_d - m_t[..., None])
        w_inter = jnp.exp(m_inter - m_t)[..., None]
        num = jnp.einsum("bhts,bhsv->bhtv", s, vc) + w_inter * jnp.einsum("bhtk,bhkv->bhtv", qc, c_mat)
        den = s.sum(-1, keepdims=True) + w_inter * jnp.einsum("bhtk,bhk->bht", qc, n_vec)[..., None]
        h = num / jnp.maximum(jnp.abs(den), jnp.exp(-m_t)[..., None])
        b_last = bcum[..., -1]
        log_w = b_last[..., None] - bcum + ic
        m_new = jnp.maximum(b_last + m_prev, log_w.max(-1))
        w = jnp.exp(log_w - m_new[..., None])
        carry_decay = jnp.exp(b_last + m_prev - m_new)
        c_mat = carry_decay[..., None, None] * c_mat + jnp.einsum("bhs,bhsk,bhsv->bhkv", w, kc, vc)
        n_vec = carry_decay[..., None] * n_vec + jnp.einsum("bhs,bhsk->bhk", w, kc)
        return (c_mat, n_vec, m_new), h

    state, h = lax.scan(step, state, tuple(_to_chunks(a) for a in xs))
    return _from_chunks(h), state


def _mlstm_mixer(p_ctx, p_lat, gate_b, gn_w):
    def prep(p):
        b, t, _ = p["a_q"].shape
        q = _heads(p["a_q"], A_HEADS)
        k = _heads(p["a_k"], A_HEADS) * A_HD ** -0.5
        v = _heads(p["a_v"], A_HEADS)
        g = p["a_gates"].astype(F32).reshape(b, t, 2, 2, A_HEADS) + gate_b
        log_i = g[..., 0, :]
        log_f = jax.nn.log_sigmoid(g[..., 1, :])
        return ((q, k, v, log_i[:, :, 0], log_f[:, :, 0]),
                (q, k, v, log_i[:, :, 1], log_f[:, :, 1]))

    b = p_lat["a_q"].shape[0]
    init = (jnp.zeros((b, A_HEADS, A_HD, A_HD), F32), jnp.zeros((b, A_HEADS, A_HD), F32),
            jnp.zeros((b, A_HEADS), F32))
    h_ctx, h_lat = _two_way(_mlstm_scan, init, prep(p_ctx), prep(p_lat))

    def post(h, p):
        hn = _head_norm(h, HEAD_EPS).reshape(h.shape[0], h.shape[1], A_DIM) * gn_w
        return (jax.nn.sigmoid(p["a_o"].astype(F32)) * hn).astype(p["a_o"].dtype)

    return post(h_ctx, p_ctx), post(h_lat, p_lat)


def _rwkv7_scan(xs, state):
    def step(s_mat, xt):
        r, w, k, v, kk, a = xt
        s_mat = (s_mat * w[:, :, None, :]
                 - jnp.einsum("bhvk,bhk->bhv", s_mat, kk)[..., None] * (kk * a)[:, :, None, :]
                 + v[..., :, None] * k[:, :, None, :])
        return s_mat, jnp.einsum("bhvk,bhk->bhv", s_mat, r)

    state, ys = lax.scan(step, state, tuple(jnp.moveaxis(a, 1, 0) for a in xs))
    return jnp.moveaxis(ys, 0, 1), state


def _rwkv7_mixer(p_ctx, p_lat, w0, w_up, a0, a_up, g_up, k_k, k_a, r_k, gn):
    def prep(p):
        r = _heads(p["b_r"], B_HEADS)
        k = _heads(p["b_k"], B_HEADS)
        v = _heads(p["b_v"], B_HEADS)
        kk = k * _heads(k_k, B_HEADS)
        kk = kk / jnp.maximum(jnp.sqrt(jnp.sum(jnp.square(kk), -1, keepdims=True)), 1e-12)
        w_codes = _split(p["b_w_code"].astype(F32), (B_W_LORA, B_W_LORA))
        a_codes = _split(p["b_a_code"].astype(F32), (B_A_LORA, B_A_LORA))
        dirs = []
        for d in range(2):
            w_log = -jax.nn.softplus(-(w0[d].astype(F32) + jnp.tanh(w_codes[d]) @ w_up[d].astype(F32))) - 0.5
            decay = jnp.exp(-jnp.exp(_heads(w_log, B_HEADS)))
            a = jax.nn.sigmoid(_heads(a0[d].astype(F32) + a_codes[d] @ a_up[d].astype(F32), B_HEADS))
            k_d = k * (1.0 + (a - 1.0) * _heads(k_a, B_HEADS))
            dirs.append((r, decay, k_d, v, kk, a))
        return (dirs[0], dirs[1])

    b = p_lat["b_r"].shape[0]
    init = jnp.zeros((b, B_HEADS, B_HD, B_HD), F32)
    d_ctx, d_lat = prep(p_ctx), prep(p_lat)
    y_ctx, y_lat = _two_way(_rwkv7_scan, init, d_ctx, d_lat)

    def post(y, p, dirs):
        rk = _heads(r_k, B_HEADS)
        bonus = ((dirs[0][0] * dirs[0][2] * rk).sum(-1, keepdims=True) * dirs[0][3]
                 + (dirs[1][0] * dirs[1][2] * rk).sum(-1, keepdims=True) * dirs[1][3])
        o = _head_norm(y, B_GN_EPS) * _heads(gn[0], B_HEADS) + _heads(gn[1], B_HEADS) + bonus
        g = jax.nn.sigmoid(p["b_g_code"].astype(F32)) @ g_up.astype(F32)
        bb, t = y.shape[:2]
        return (o.reshape(bb, t, B_DIM) * g).astype(p["b_g_code"].dtype)

    return post(y_ctx, p_ctx, d_ctx), post(y_lat, p_lat, d_lat)


def _gla_scan(xs, state):
    tril = jnp.tril(jnp.ones((CHUNK, CHUNK), dtype=bool))

    def step(s_mat, xc):
        qc, kc, vc, gc = xc
        bc = jnp.cumsum(gc, axis=2)
        rel = jnp.where(tril[:, :, None], bc[:, :, :, None, :] - bc[:, :, None, :, :], -jnp.inf)
        att = jnp.einsum("bhtd,bhsd,bhtsd->bhts", qc, kc, jnp.exp(rel))
        o = (jnp.einsum("bhts,bhsv->bhtv", att, vc)
             + jnp.einsum("bhtd,bhdv->bhtv", qc * jnp.exp(bc), s_mat))
        b_last = bc[:, :, -1:, :]
        s_mat = (jnp.exp(b_last[:, :, 0, :])[..., None] * s_mat
                 + jnp.einsum("bhsd,bhsv->bhdv", kc * jnp.exp(b_last - bc), vc))
        return s_mat, o

    state, o = lax.scan(step, state, tuple(_to_chunks(a) for a in xs))
    return _from_chunks(o), state


def _gla_mixer(p_ctx, p_lat, gate_up, gate_b, gn_w):
    def prep(p):
        q = _heads(p["c_q"], C_HEADS) * C_HDK ** -0.5
        k = _heads(p["c_k"], C_HEADS)
        v = _heads(p["c_v"], C_HEADS)
        codes = _split(p["c_g_code"].astype(F32), (C_G_LORA, C_G_LORA))
        lg = [_heads(jax.nn.log_sigmoid(codes[d] @ gate_up[d].astype(F32) + gate_b[d]) / C_TAU, C_HEADS)
              for d in range(2)]
        return ((q, k, v, lg[0]), (q, k, v, lg[1]))

    b = p_lat["c_q"].shape[0]
    init = jnp.zeros((b, C_HEADS, C_HDK, C_HDV), F32)
    o_ctx, o_lat = _two_way(_gla_scan, init, prep(p_ctx), prep(p_lat))

    def post(o, p):
        on = _head_norm(o, HEAD_EPS, rms=True).reshape(o.shape[0], o.shape[1], C_DV) * gn_w
        return (on * jax.nn.silu(p["c_o"].astype(F32))).astype(p["c_o"].dtype)

    return post(o_ctx, p_ctx), post(o_lat, p_lat)


def _token_mixer(u_ctx, u_lat, w_in, conv_k, mlstm_gate_b, mlstm_gn, rwkv_w0, rwkv_w_up, rwkv_a0,
                 rwkv_a_up, rwkv_g_up, rwkv_kk, rwkv_ka, rwkv_rk, rwkv_gn, gla_gate_up, gla_gate_b,
                 gla_gn, w_branch_a, w_branch_b, w_branch_c, w_out, need_ctx):
    p_ctx = _in_proj(u_ctx, w_in, conv_k, u_ctx.shape[1])
    p_lat = _in_proj(u_lat, w_in, conv_k, GRID_W)
    h_a = _mlstm_mixer(p_ctx, p_lat, mlstm_gate_b, mlstm_gn)
    h_b = _rwkv7_mixer(p_ctx, p_lat, rwkv_w0, rwkv_w_up, rwkv_a0, rwkv_a_up, rwkv_g_up,
                       rwkv_kk, rwkv_ka, rwkv_rk, rwkv_gn)
    h_c = _gla_mixer(p_ctx, p_lat, gla_gate_up, gla_gate_b, gla_gn)

    def merge(p, ha, hb, hc):
        g_a, g_b, g_c = _split(jax.nn.sigmoid(p["merge_gates"]), (D_MODEL, D_MODEL, D_MODEL))
        return (g_a * (ha @ w_branch_a) + g_b * (hb @ w_branch_b) + g_c * (hc @ w_branch_c)) @ w_out

    y_lat = merge(p_lat, h_a[1], h_b[1], h_c[1])
    y_ctx = merge(p_ctx, h_a[0], h_b[0], h_c[0]) if need_ctx else None
    return y_ctx, y_lat


def _swiglu(u, wg, wu, wd):
    return (jax.nn.silu(u @ wg) * (u @ wu)) @ wd


def _moe(u, router, router_b, wg, wu, wd):
    logits = (u @ router).astype(F32) + router_b.astype(F32)
    top_v, top_i = lax.top_k(logits, TOP_K)
    probs = jax.nn.softmax(top_v, axis=-1)
    comb = jnp.einsum("btk,btke->bte", probs, jax.nn.one_hot(top_i, N_EXPERTS, dtype=F32)).astype(u.dtype)
    out = jnp.zeros(u.shape[:-1] + (wd.shape[-1],), u.dtype)
    for e in range(N_EXPERTS):
        out = out + comb[..., e:e + 1] * _swiglu(u, wg[e], wu[e], wd[e])
    return out


def _channel_mixer(layer, u, ffn_w_gate, ffn_w_up, ffn_w_down, moe_router, moe_router_b,
                   moe_w_gate, moe_w_up, moe_w_down):
    i = layer // 2
    if layer % 2 == 0:
        return _swiglu(u, ffn_w_gate[i], ffn_w_up[i], ffn_w_down[i])
    return _moe(u, moe_router[i], moe_router_b[i], moe_w_gate[i], moe_w_up[i], moe_w_down[i])


def setup_inputs(seed: int = 0) -> dict:
    key = jax.random.key(seed)
    ks = iter(jax.random.split(key, 48))
    D = D_MODEL

    def nrm(shape, scale):
        return scale * jax.random.normal(next(ks), shape, F32)

    x = nrm((BATCH, SEQ, D), 1.0)
    c = nrm((BATCH, D), 1.0)
    ctx = nrm((BATCH, CTX_LEN, D), 1.0)
    c_ctx = nrm((D,), 1.0)
    w_mod = nrm((DEPTH, D, 6 * D), 0.5 * D ** -0.5)
    b_mod = nrm((DEPTH, 6 * D), 0.02)
    w_in = nrm((DEPTH, D, N_IN), D ** -0.5)
    conv_k = nrm((DEPTH, 3, 3, CONV_W), 0.15).at[:, 1, 1, :].add(1.0)
    mlstm_gate_b = jnp.stack(
        [nrm((DEPTH, 2, A_HEADS), 0.1),
         3.0 + 3.0 * jax.random.uniform(next(ks), (DEPTH, 2, A_HEADS), F32)], axis=2)
    mlstm_gn = 1.0 + nrm((DEPTH, A_DIM), 0.05)
    rwkv_w0 = jax.random.uniform(next(ks), (DEPTH, 2, B_DIM), F32, -6.0, 1.0)
    rwkv_w_up = nrm((DEPTH, 2, B_W_LORA, B_DIM), 0.5 * B_W_LORA ** -0.5)
    rwkv_a0 = nrm((DEPTH, 2, B_DIM), 0.5)
    rwkv_a_up = nrm((DEPTH, 2, B_A_LORA, B_DIM), 0.5 * B_A_LORA ** -0.5)
    rwkv_g_up = nrm((DEPTH, B_G_LORA, B_DIM), B_G_LORA ** -0.5)
    rwkv_kk = 0.85 + nrm((DEPTH, B_DIM), 0.1)
    rwkv_ka = 1.0 + nrm((DEPTH, B_DIM), 0.1)
    rwkv_rk = nrm((DEPTH, B_DIM), 0.1)
    rwkv_gn = jnp.stack([1.0 + nrm((DEPTH, B_DIM), 0.05), nrm((DEPTH, B_DIM), 0.02)], axis=1)
    gla_gate_up = nrm((DEPTH, 2, C_G_LORA, C_DK), C_G_LORA ** -0.5)
    gla_gate_b = nrm((DEPTH, 2, C_DK), 0.5)
    gla_gn = 1.0 + nrm((DEPTH, C_DV), 0.05)
    w_branch_a = nrm((DEPTH, A_DIM, D), BETA * A_DIM ** -0.5)
    w_branch_b = nrm((DEPTH, B_DIM, D), BETA * B_DIM ** -0.5)
    w_branch_c = nrm((DEPTH, C_DV, D), BETA * C_DV ** -0.5)
    w_out = nrm((DEPTH, D, D), BETA * D ** -0.5)
    ln_post = jnp.stack([1.0 + nrm((DEPTH, 2, D), 0.05), nrm((DEPTH, 2, D), 0.02)], axis=2)
    ffn_w_gate = nrm((N_DENSE, D, D_FF), D ** -0.5)
    ffn_w_up = nrm((N_DENSE, D, D_FF), D ** -0.5)
    ffn_w_down = nrm((N_DENSE, D_FF, D), BETA * D_FF ** -0.5)
    moe_router = nrm((N_MOE, D, N_EXPERTS), D ** -0.5)
    moe_router_b = nrm((N_MOE, N_EXPERTS), 0.01)
    moe_w_gate = nrm((N_MOE, N_EXPERTS, D, D_FF), D ** -0.5)
    moe_w_up = nrm((N_MOE, N_EXPERTS, D, D_FF), D ** -0.5)
    moe_w_down = nrm((N_MOE, N_EXPERTS, D_FF, D), BETA * D_FF ** -0.5)
    return {"x": x, "c": c, "ctx": ctx, "c_ctx": c_ctx, "w_mod": w_mod, "b_mod": b_mod,
            "w_in": w_in, "conv_k": conv_k, "mlstm_gate_b": mlstm_gate_b, "mlstm_gn": mlstm_gn,
            "rwkv_w0": rwkv_w0, "rwkv_w_up": rwkv_w_up, "rwkv_a0": rwkv_a0, "rwkv_a_up": rwkv_a_up,
            "rwkv_g_up": rwkv_g_up, "rwkv_kk": rwkv_kk, "rwkv_ka": rwkv_ka, "rwkv_rk": rwkv_rk,
            "rwkv_gn": rwkv_gn, "gla_gate_up": gla_gate_up, "gla_gate_b": gla_gate_b, "gla_gn": gla_gn,
            "w_branch_a": w_branch_a, "w_branch_b": w_branch_b, "w_branch_c": w_branch_c,
            "w_out": w_out, "ln_post": ln_post, "ffn_w_gate": ffn_w_gate, "ffn_w_up": ffn_w_up,
            "ffn_w_down": ffn_w_down, "moe_router": moe_router, "moe_router_b": moe_router_b,
            "moe_w_gate": moe_w_gate, "moe_w_up": moe_w_up, "moe_w_down": moe_w_down}


def reference(x, c, ctx, c_ctx, w_mod, b_mod, w_in, conv_k, mlstm_gate_b, mlstm_gn, rwkv_w0,
              rwkv_w_up, rwkv_a0, rwkv_a_up, rwkv_g_up, rwkv_kk, rwkv_ka, rwkv_rk, rwkv_gn,
              gla_gate_up, gla_gate_b, gla_gn, w_branch_a, w_branch_b, w_branch_c, w_out, ln_post,
              ffn_w_gate, ffn_w_up, ffn_w_down, moe_router, moe_router_b, moe_w_gate, moe_w_up,
              moe_w_down):
    six = (D_MODEL,) * 6
    silu_c = jax.nn.silu(c)[:, None, :]
    silu_cc = jax.nn.silu(c_ctx)[None, None, :]
    for l in range(DEPTH):
        need_ctx = l < DEPTH - 1
        m_lat = _split(silu_c @ w_mod[l] + b_mod[l], six)
        m_ctx = _split(silu_cc @ w_mod[l] + b_mod[l], six)
        u_lat = _modulate(x, m_lat[0], m_lat[1])
        u_ctx = _modulate(ctx, m_ctx[0], m_ctx[1])
        y_ctx, y_lat = _token_mixer(
            u_ctx, u_lat, w_in[l], conv_k[l], mlstm_gate_b[l], mlstm_gn[l], rwkv_w0[l], rwkv_w_up[l],
            rwkv_a0[l], rwkv_a_up[l], rwkv_g_up[l], rwkv_kk[l], rwkv_ka[l], rwkv_rk[l], rwkv_gn[l],
            gla_gate_up[l], gla_gate_b[l], gla_gn[l], w_branch_a[l], w_branch_b[l], w_branch_c[l],
            w_out[l], need_ctx)
        x = _post_norm(ALPHA * x + m_lat[2] * y_lat, ln_post[l, 0])
        if need_ctx:
            ctx = _post_norm(ALPHA * ctx + m_ctx[2] * y_ctx, ln_post[l, 0])
        f_lat = _channel_mixer(l, _modulate(x, m_lat[3], m_lat[4]), ffn_w_gate, ffn_w_up, ffn_w_down,
                               moe_router, moe_router_b, moe_w_gate, moe_w_up, moe_w_down)
        x = _post_norm(ALPHA * x + m_lat[5] * f_lat, ln_post[l, 1])
        if need_ctx:
            f_ctx = _channel_mixer(l, _modulate(ctx, m_ctx[3], m_ctx[4]), ffn_w_gate, ffn_w_up, ffn_w_down,
                                   moe_router, moe_router_b, moe_w_gate, moe_w_up, moe_w_down)
            ctx = _post_norm(ALPHA * ctx + m_ctx[5] * f_ctx, ln_post[l, 1])
    return x
```

```python
import functools
import math

import jax
import jax.numpy as jnp
import numpy as np
from jax import lax
from jax.experimental import pallas as pl
from jax.experimental.pallas import tpu as pltpu

F32 = jnp.float32
BF16 = jnp.bfloat16

D = 2048
GRID_W = 64
A_H, A_HD, A_HP = 4, 192, 256
A_W = A_H * A_HP
B_H, B_HD = 10, 64
B_W = B_H * B_HD
B_W_LORA, B_A_LORA, B_G_LORA = 64, 32, 128
B_GN_EPS = 64e-5
C_H, C_DK, C_DV = 4, 80, 160
C_KW, C_KP, C_VW = C_H * C_DK, 384, C_H * C_DV
C_G_LORA = 16
C_TAU = 16.0
D_FF, D_FFP = 5504, 5632
N_EXPERTS = 8
LN_EPS = 1e-6
HEAD_EPS = 1e-5
LANE = 128

ZC_AQ, ZC_AK, ZC_AV = 0, 1024, 2048
ZC_BR, ZC_BK, ZC_BV, ZC_CV = 3200, 3840, 4480, 5120
ZC_CQ, ZC_CK = 5760, 6144
ZC_W = 6528
ZR_MG, ZR_AO, ZR_AG, ZR_BW, ZR_BA, ZR_BG, ZR_CO, ZR_CG = 0, 6144, 7168, 7296, 7424, 7552, 7680, 8320
ZR_W = 8448
H_W = A_W + B_W + C_VW

TM = 512
TM_S = 256
LCH = 128
RC = 16
VMEM_LIMIT = 56 * 1024 * 1024


def _cparams(sem):
    return pltpu.CompilerParams(dimension_semantics=sem, vmem_limit_bytes=VMEM_LIMIT)


def _ln(a):
    mu = jnp.mean(a, axis=-1, keepdims=True)
    d = a - mu
    var = jnp.mean(d * d, axis=-1, keepdims=True)
    return d * lax.rsqrt(var + LN_EPS)


def _sigmoid(x):
    return 1.0 / (1.0 + jnp.exp(-x))


def _softplus(x):
    return jnp.maximum(x, 0.0) + jnp.log(1.0 + jnp.exp(-jnp.abs(x)))


def _bdot(a, b):
    return jnp.dot(a.astype(BF16), b.astype(BF16), preferred_element_type=F32)


def _dot_nt(a, b):
    return lax.dot_general(a.astype(BF16), b.astype(BF16), (((1,), (1,)), ((), ())), preferred_element_type=F32)


def _dot_tn(a, b):
    return lax.dot_general(a.astype(BF16), b.astype(BF16), (((0,), (0,)), ((), ())), preferred_element_type=F32)


def _split2(x):
    x1 = x.astype(BF16)
    return x1, (x - x1.astype(F32)).astype(BF16)


def _split3(x):
    x1 = x.astype(BF16)
    r = x - x1.astype(F32)
    x2 = r.astype(BF16)
    return x1, x2, (r - x2.astype(F32)).astype(BF16)


def _sel_dot(m, x):
    return sum(jnp.dot(m, p, preferred_element_type=F32) for p in _split3(x))


def _dot_sel(x, m):
    return sum(jnp.dot(p, m, preferred_element_type=F32) for p in _split2(x))


def _dot3(a, b):
    a1, a2 = _split2(a)
    b1, b2 = _split2(b)
    return (jnp.dot(a1, b1, preferred_element_type=F32) + jnp.dot(a1, b2, preferred_element_type=F32)
            + jnp.dot(a2, b1, preferred_element_type=F32))


def _dir_pos(n):
    return np.stack([np.arange(n), n - 1 - np.arange(n)])


def _mlstm_consts():
    pos = _dir_pos(LCH)
    inc = (pos[:, None, :] <= pos[:, :, None]).astype(np.float32)
    return jnp.asarray(inc, BF16)


def _gla_consts():
    L = LCH
    pos = _dir_pos(L)
    widths = [L >> i for i in range(int(math.log2(L)))]
    nl = len(widths)
    wmat = np.zeros((2, (2 * nl + 2) * L + 8, L), np.float32)
    masks = np.zeros((2, (nl + 1) * C_H * L, L), np.float32)
    for d in range(2):
        p = pos[d]
        for li, w in enumerate(widths):
            half = w // 2
            blk, off = p // w, p % w
            mid = blk * w + half
            isq, isk = off >= half, off < half
            eq = (p[None, :] >= mid[:, None]) & (p[None, :] <= p[:, None]) & isq[:, None]
            ek = (p[None, :] > p[:, None]) & (p[None, :] <= mid[:, None] - 1) & isk[:, None]
            wmat[d, li * L:(li + 1) * L] = eq
            wmat[d, (nl + li) * L:(nl + li + 1) * L] = ek
            m = (blk[:, None] == blk[None, :]) & isq[:, None] & isk[None, :]
            masks[d, li * C_H * L:(li + 1) * C_H * L] = np.tile(m, (C_H, 1))
        wmat[d, 2 * nl * L:(2 * nl + 1) * L] = p[None, :] <= p[:, None]
        wmat[d, (2 * nl + 1) * L:(2 * nl + 2) * L] = p[None, :] > p[:, None]
        wmat[d, (2 * nl + 2) * L:] = 1.0
        masks[d, nl * C_H * L:] = np.tile(np.eye(L), (C_H, 1))
    qmask = np.zeros((C_H, 1, C_KP), np.float32)
    vmask = np.zeros((C_H, 1, C_VW), np.float32)
    bd = np.zeros((C_VW, C_KP), np.float32)
    for h in range(C_H):
        qmask[h, 0, h * C_DK:(h + 1) * C_DK] = 1.0
        vmask[h, 0, h * C_DV:(h + 1) * C_DV] = 1.0
        bd[h * C_DV:(h + 1) * C_DV, h * C_DK:(h + 1) * C_DK] = 1.0
    return (jnp.asarray(wmat, BF16), jnp.asarray(masks, F32), jnp.asarray(qmask, F32), jnp.asarray(vmask, F32),
            jnp.asarray(bd, F32), nl)


def _rwkv_consts():
    c, H = RC, B_H
    pos = _dir_pos(c)
    inc = (pos[:, None, :] <= pos[:, :, None]).astype(np.float32)
    hh = np.repeat(np.arange(H), c)
    same = hh[:, None] == hh[None, :]
    sm = np.zeros((2, H * c, H * c), np.float32)
    im = np.zeros((2, H * c, H * c), np.float32)
    for d in range(2):
        pp = np.tile(pos[d], H)
        sm[d] = same & (pp[None, :] < pp[:, None])
        im[d] = same & (pp[None, :] <= pp[:, None])
    lane_head = np.repeat(np.arange(H), B_HD)
    stackmask = (hh[:, None] == lane_head[None, :]).astype(np.float32)
    fold = np.tile(np.eye(B_HD, dtype=np.float32), (H, 1))
    sel = np.tile(np.eye(c, dtype=np.float32), (1, H))
    seg = (lane_head[:, None] == lane_head[None, :]).astype(np.float32)
    e0 = np.zeros((8, B_HD), np.float32)
    e0[0] = 1.0
    return dict(inc=jnp.asarray(inc, BF16), sm=jnp.asarray(sm, F32), im=jnp.asarray(im, F32),
                stackmask=jnp.asarray(stackmask, F32), fold=jnp.asarray(fold, BF16),
                unfold=jnp.asarray(fold.T.copy(), BF16), sel=jnp.asarray(sel, BF16), seg=jnp.asarray(seg, BF16),
                ones8=jnp.ones((8, c), BF16), e0=jnp.asarray(e0, BF16), eye=jnp.eye(H * c, dtype=F32))


def _seg_consts():
    la = np.arange(A_W)
    va = (la % A_HP) < A_HD
    sega = ((la[:, None] // A_HP) == (la[None, :] // A_HP)) & va[:, None] & va[None, :]
    lb = np.arange(B_W) // B_HD
    segb = lb[:, None] == lb[None, :]
    lc = np.arange(C_VW) // C_DV
    segc = lc[:, None] == lc[None, :]
    return jnp.asarray(sega, BF16), jnp.asarray(segb, BF16), jnp.asarray(segc, BF16)


def _pad_last(a, n):
    return jnp.pad(a, [(0, 0)] * (a.ndim - 1) + [(0, n - a.shape[-1])])


def _pad_heads(a, nh, hd, hp):
    a = a.reshape(a.shape[:-1] + (nh, hd))
    a = jnp.pad(a, [(0, 0)] * (a.ndim - 1) + [(0, hp - hd)])
    return a.reshape(a.shape[:-2] + (nh * hp,))


def _pack_conv_cols(w):
    s = lambda a, b: w[..., a:b]
    z = lambda n: jnp.zeros(w.shape[:-1] + (n,), w.dtype)
    parts = [_pad_heads(s(0, 768), A_H, A_HD, A_HP), _pad_heads(s(768, 1536), A_H, A_HD, A_HP),
             _pad_heads(s(1536, 2304), A_H, A_HD, A_HP), z(128),
             s(2304, 2944), s(2944, 3584), s(3584, 4224), s(4864, 5504),
             _pad_last(s(4224, 4544), C_KP), _pad_last(s(4544, 4864), C_KP)]
    return jnp.concatenate(parts, axis=-1)


def _pack_rest_cols(w):
    s = lambda a, b: w[..., a:b]
    parts = [s(0, 6144), _pad_heads(s(6144, 6912), A_H, A_HD, A_HP), _pad_last(s(6912, 6928), LANE),
             s(6928, 7056), _pad_last(s(7056, 7120), LANE), s(7120, 7248), s(7280, 7920),
             _pad_last(s(7248, 7280), LANE)]
    return jnp.concatenate(parts, axis=-1)


def _mod_kernel(c_ref, w_ref, b_ref, o_ref):
    c = c_ref[...]
    s = c * _sigmoid(c)
    o_ref[...] = jnp.dot(s, w_ref[...], preferred_element_type=F32) + b_ref[...]


def _mod_call(cvec, w_mod, b_mod):
    depth, _, n6 = w_mod.shape
    tn = 1536
    return pl.pallas_call(
        _mod_kernel, out_shape=jax.ShapeDtypeStruct((depth, 8, n6), F32), grid=(depth, n6 // tn),
        in_specs=[pl.BlockSpec((8, D), lambda l, j: (0, 0)),
                  pl.BlockSpec((None, D, tn), lambda l, j: (l, 0, j)),
                  pl.BlockSpec((None, 1, tn), lambda l, j: (l, 0, j))],
        out_specs=pl.BlockSpec((None, 8, tn), lambda l, j: (l, 0, j)),
        compiler_params=_cparams(("parallel", "parallel")), name="mod_vectors",
    )(cvec, w_mod, b_mod.reshape(depth, 1, n6))


class _Geom:
    def __init__(self, batch, seq, ctx_len):
        assert seq % TM == 0 and (batch * ctx_len) % TM == 0 and TM % ctx_len == 0 and TM % GRID_W == 0
        assert ctx_len % LCH == 0 and seq % LCH == 0 and ctx_len & (ctx_len - 1) == 0
        self.batch, self.seq, self.ctx_len = batch, seq, ctx_len
        self.rows = batch * (seq + ctx_len)
        self.n_lat_rows = batch * seq

    def mod_row(self, i, tm):
        return jnp.minimum(i // (self.seq // tm), self.batch)

    def mod_spec(self, layer, k, tm):
        return pl.BlockSpec((None, 1, D), lambda i, *_: (layer * 8 + self.mod_row(i, tm), 0, k))

    def chunk_block(self, b, d, j):
        ncc, nlc = self.ctx_len // LCH, self.seq // LCH
        is_ctx = j < ncc
        jj = jnp.where(is_ctx, j, j - ncc)
        n = jnp.where(is_ctx, ncc, nlc)
        loc = jnp.where(d == 0, jj, n - 1 - jj)
        return jnp.where(is_ctx, self.n_lat_rows // LCH + b * ncc + loc, b * nlc + loc)

    @property
    def chunk_steps(self):
        return (self.ctx_len + self.seq) // LCH


def _inproj_kernel(x_ref, sh_ref, sc_ref, w_ref, o_ref, u_ref):
    @pl.when(pl.program_id(1) == 0)
    def _():
        u_ref[...] = (_ln(x_ref[...]) * (1.0 + sc_ref[...]) + sh_ref[...]).astype(BF16)

    o_ref[...] = jnp.dot(u_ref[...], w_ref[...], preferred_element_type=F32)


def _inproj_call(g, x, modflat, layer, w, tn, name):
    n = w.shape[-1]
    return pl.pallas_call(
        _inproj_kernel, out_shape=jax.ShapeDtypeStruct((g.rows, n), F32), grid=(g.rows // TM, n // tn),
        in_specs=[pl.BlockSpec((TM, D), lambda i, j: (i, 0)),
                  g.mod_spec(layer, 0, TM), g.mod_spec(layer, 1, TM),
                  pl.BlockSpec((None, D, tn), lambda i, j: (layer, 0, j))],
        out_specs=pl.BlockSpec((TM, tn), lambda i, j: (i, j)),
        scratch_shapes=[pltpu.VMEM((TM, D), BF16)],
        compiler_params=_cparams(("parallel", "arbitrary")), name=name,
    )(x, modflat, modflat, w)


def _conv_kernel(prev_ref, main_ref, next_ref, w_ref, o_ref, *, n_lat_tiles, tiles_per_batch, ctx_len):
    i = pl.program_id(0)
    is_ctx = i >= n_lat_tiles
    ib = i % tiles_per_batch
    top_ok = jnp.logical_and(ib != 0, jnp.logical_not(is_ctx))
    bot_ok = jnp.logical_and(ib != tiles_per_batch - 1, jnp.logical_not(is_ctx))
    t = lax.broadcasted_iota(jnp.int32, (TM, 1), 0)
    main = main_ref[...]
    up = jnp.concatenate([prev_ref[...], main[:TM - GRID_W]], axis=0)
    dn = jnp.concatenate([main[GRID_W:], next_ref[...]], axis=0)
    lat = jnp.logical_not(is_ctx)
    up = jnp.where(jnp.logical_and(lat, jnp.logical_or(t >= GRID_W, top_ok)), up, 0.0)
    dn = jnp.where(jnp.logical_and(lat, jnp.logical_or(t < TM - GRID_W, bot_ok)), dn, 0.0)
    w = w_ref[...]
    a0 = w[0:1] * up + w[3:4] * main + w[6:7] * dn
    a1 = w[1:2] * up + w[4:5] * main + w[7:8] * dn
    a2 = w[2:3] * up + w[5:6] * main + w[8:9] * dn
    col = jnp.where(is_ctx, t & (ctx_len - 1), t & (GRID_W - 1))
    last = jnp.where(is_ctx, ctx_len - 1, GRID_W - 1)
    o_ref[...] = (a1 + jnp.where(col != 0, pltpu.roll(a0, 1, 0), 0.0)
                  + jnp.where(col != last, pltpu.roll(a2, TM - 1, 0), 0.0))


def _conv_call(g, z, conv_w, layer):
    tc = 2176
    hb = TM // GRID_W
    nb64 = g.rows // GRID_W
    kern = functools.partial(_conv_kernel, n_lat_tiles=g.n_lat_rows // TM, tiles_per_batch=g.seq // TM,
                             ctx_len=g.ctx_len)
    return pl.pallas_call(
        kern, out_shape=jax.ShapeDtypeStruct((g.rows, ZC_W), F32), grid=(g.rows // TM, ZC_W // tc),
        in_specs=[pl.BlockSpec((GRID_W, tc), lambda i, j: (jnp.maximum(i * hb - 1, 0), j)),
                  pl.BlockSpec((TM, tc), lambda i, j: (i, j)),
                  pl.BlockSpec((GRID_W, tc), lambda i, j: (jnp.minimum(i * hb + hb, nb64 - 1), j)),
                  pl.BlockSpec((None, 9, tc), lambda i, j: (layer, 0, j))],
        out_specs=pl.BlockSpec((TM, tc), lambda i, j: (i, j)),
        compiler_params=_cparams(("parallel", "parallel")), name="grid_conv",
    )(z, z, z, conv_w)


def _mlstm_kernel(q_ref, k_ref, v_ref, g_ref, gb_ref, inc_ref, eye_ref, y_ref, c_ref, n_ref, m_ref):
    d = pl.program_id(1)
    j = pl.program_id(2)

    @pl.when(j == 0)
    def _():
        c_ref[...] = jnp.zeros_like(c_ref)
        n_ref[...] = jnp.zeros_like(n_ref)
        m_ref[...] = jnp.zeros_like(m_ref)

    L = LCH
    inc = inc_ref[...]
    g = g_ref[...] + gb_ref[...]
    lf = -_softplus(-g)
    b_all = _sel_dot(inc, lf)
    r_all = g - pltpu.roll(b_all, LANE - A_H, 1)
    rt_all = sum(lax.dot_general(eye_ref[...], p, (((1,), (1,)), ((), ())), preferred_element_type=F32)
                 for p in _split3(r_all))
    is_f = d == 0
    valid = inc > 0
    lane = lax.broadcasted_iota(jnp.int32, (1, LANE), 1)
    m_all = m_ref[...]
    outs = []
    for h in range(A_H):
        li = jnp.where(is_f, g[:, h:h + 1], g[:, 8 + h:9 + h])
        b = jnp.where(is_f, b_all[:, 4 + h:5 + h], b_all[:, 12 + h:13 + h])
        r_row = jnp.where(is_f, rt_all[h:h + 1, :], rt_all[8 + h:9 + h, :])
        b_last = jnp.where(is_f, b_all[L - 1:L, 4 + h:5 + h], b_all[0:1, 12 + h:13 + h])
        m_prev = m_all[:, h:h + 1]
        log_d = jnp.where(valid, b + r_row, -jnp.inf)
        m_inter = b + m_prev
        m_t = jnp.maximum(m_inter, jnp.max(log_d, axis=-1, keepdims=True))
        qh = q_ref[:, h * A_HP:(h + 1) * A_HP]
        kh = k_ref[:, h * A_HP:(h + 1) * A_HP] * (A_HD ** -0.5)
        vh = v_ref[:, h * A_HP:(h + 1) * A_HP]
        s = _dot_nt(qh, kh) * jnp.exp(log_d - m_t)
        w_inter = jnp.exp(m_inter - m_t)
        num = _bdot(s, vh) + w_inter * _bdot(qh, c_ref[h])
        den = jnp.sum(s, axis=-1, keepdims=True) + w_inter * jnp.sum(qh * n_ref[h], axis=-1, keepdims=True)
        outs.append(num / jnp.maximum(jnp.abs(den), jnp.exp(-m_t)))
        log_w = b_last - b + li
        m_new = jnp.maximum(b_last + m_prev, jnp.max(log_w, axis=0, keepdims=True))
        kw = kh * jnp.exp(log_w - m_new)
        cd = jnp.exp(b_last + m_prev - m_new)
        c_ref[h] = cd * c_ref[h] + _dot_tn(kw, vh)
        n_ref[h] = cd * n_ref[h] + jnp.sum(kw, axis=0, keepdims=True)
        m_all = jnp.where(lane == h, m_new, m_all)
    m_ref[...] = m_all
    y_ref[...] = jnp.concatenate(outs, axis=-1)


def _mlstm_call(g, zc, zr, gate_b, layer, inc):
    blk = g.chunk_block
    qkv = lambda c: pl.BlockSpec((LCH, A_W), lambda b, d, j: (blk(b, d, j), c))
    return pl.pallas_call(
        _mlstm_kernel, out_shape=jax.ShapeDtypeStruct((2, g.rows, A_W), F32),
        grid=(g.batch, 2, g.chunk_steps),
        in_specs=[qkv(ZC_AQ // A_W), qkv(ZC_AK // A_W), qkv(ZC_AV // A_W),
                  pl.BlockSpec((LCH, LANE), lambda b, d, j: (blk(b, d, j), ZR_AG // LANE)),
                  pl.BlockSpec((None, 1, LANE), lambda b, d, j: (layer, 0, 0)),
                  pl.BlockSpec((None, LCH, LCH), lambda b, d, j: (d, 0, 0)),
                  pl.BlockSpec((LANE, LANE), lambda b, d, j: (0, 0))],
        out_specs=pl.BlockSpec((None, LCH, A_W), lambda b, d, j: (d, blk(b, d, j), 0)),
        scratch_shapes=[pltpu.VMEM((A_H, A_HP, A_HP), F32), pltpu.VMEM((A_H, 1, A_HP), F32),
                        pltpu.VMEM((1, LANE), F32)],
        compiler_params=_cparams(("arbitrary", "arbitrary", "arbitrary")), name="mlstm_scan",
    )(zc, zc, zc, zr, gate_b, inc, jnp.eye(LANE, dtype=BF16))


def _gla_kernel(q_ref, k_ref, v_ref, code_ref, gup_ref, gb_ref, w_ref, mask_ref, qm_ref, vm_ref, bd_ref,
                y_ref, st_ref, *, nl):
    j = pl.program_id(2)

    @pl.when(j == 0)
    def _():
        st_ref[...] = jnp.zeros_like(st_ref)

    L = LCH
    x = _bdot(code_ref[...], gup_ref[...]) + gb_ref[...]
    glog = -_softplus(-x) * (1.0 / C_TAU)
    e = _sel_dot(w_ref[...], glog)
    q = q_ref[...] * (C_DK ** -0.5)
    k = k_ref[...]
    v = v_ref[...]
    stack = lambda a: jnp.concatenate([a * qm_ref[h] for h in range(C_H)], axis=0)
    att = jnp.zeros((C_H * L, L), F32)
    for lv in range(nl):
        qs = q * jnp.exp(e[lv * L:(lv + 1) * L])
        ks = k * jnp.exp(e[(nl + lv) * L:(nl + lv + 1) * L])
        att = att + mask_ref[lv * C_H * L:(lv + 1) * C_H * L, :] * _dot_nt(stack(qs), ks)
    att = att + mask_ref[nl * C_H * L:(nl + 1) * C_H * L, :] * _dot_nt(stack(q), k)
    o = _dot_nt(q * jnp.exp(e[2 * nl * L:(2 * nl + 1) * L]), st_ref[...])
    for h in range(C_H):
        o = o + vm_ref[h] * _bdot(att[h * L:(h + 1) * L], v)
    y_ref[...] = o
    ke = k * jnp.exp(e[(2 * nl + 1) * L:(2 * nl + 2) * L])
    tot = e[(2 * nl + 2) * L:(2 * nl + 2) * L + 1]
    st_ref[...] = st_ref[...] * jnp.exp(tot) + bd_ref[...] * _dot_tn(v, ke)


def _gla_call(g, zc, zr, gate_up, gate_b, layer, consts):
    wmat, masks, qmask, vmask, bd, nl = consts
    blk = g.chunk_block
    const = lambda a: pl.BlockSpec(a.shape, lambda b, d, j: (0,) * a.ndim)
    return pl.pallas_call(
        functools.partial(_gla_kernel, nl=nl), out_shape=jax.ShapeDtypeStruct((2, g.rows, C_VW), F32),
        grid=(g.batch, 2, g.chunk_steps),
        in_specs=[pl.BlockSpec((LCH, C_KP), lambda b, d, j: (blk(b, d, j), ZC_CQ // C_KP)),
                  pl.BlockSpec((LCH, C_KP), lambda b, d, j: (blk(b, d, j), ZC_CK // C_KP)),
                  pl.BlockSpec((LCH, C_VW), lambda b, d, j: (blk(b, d, j), ZC_CV // C_VW)),
                  pl.BlockSpec((LCH, LANE), lambda b, d, j: (blk(b, d, j), ZR_CG // LANE)),
                  pl.BlockSpec((None, None, LANE, C_KP), lambda b, d, j: (layer, d, 0, 0)),
                  pl.BlockSpec((None, None, 1, C_KP), lambda b, d, j: (layer, d, 0, 0)),
                  pl.BlockSpec((None,) + wmat.shape[1:], lambda b, d, j: (d, 0, 0)),
                  pl.BlockSpec((None,) + masks.shape[1:], lambda b, d, j: (d, 0, 0)),
                  const(qmask), const(vmask), const(bd)],
        out_specs=pl.BlockSpec((None, LCH, C_VW), lambda b, d, j: (d, blk(b, d, j), 0)),
        scratch_shapes=[pltpu.VMEM((C_VW, C_KP), F32)],
        compiler_params=_cparams(("arbitrary", "arbitrary", "arbitrary")), name="gla_scan",
    )(zc, zc, zc, zr, gate_up, gate_b, wmat, masks, qmask, vmask, bd)


def _rwkv_kernel(r_ref, k_ref, v_ref, wc_ref, ac_ref, w0_ref, wup_ref, a0_ref, aup_ref, kk_ref, ka_ref, rk_ref,
                 inc_ref, sm_ref, im_ref, stk_ref, fold_ref, unfold_ref, sel_ref, seg_ref, ones_ref, e0_ref,
                 eye_ref, y_ref, bonus_ref, st_ref, buf_ref):
    d = pl.program_id(1)
    j = pl.program_id(2)

    @pl.when(j == 0)
    def _():
        st_ref[...] = jnp.zeros_like(st_ref)

    r = r_ref[...]
    k = k_ref[...]
    v = v_ref[...]
    seg = seg_ref[...]
    kk = k * kk_ref[...]
    kk = kk / jnp.maximum(jnp.sqrt(_dot_sel(kk * kk, seg)), 1e-12)
    w_log = -_softplus(-(w0_ref[...] + _bdot(jnp.tanh(wc_ref[...]), wup_ref[...]))) - 0.5
    lw = -jnp.exp(w_log)
    a = _sigmoid(a0_ref[...] + _bdot(ac_ref[...], aup_ref[...]))
    kd = k * (1.0 + (a - 1.0) * ka_ref[...])
    bonus_ref[...] = _dot_sel(r * kd * rk_ref[...], seg) * v
    buf_ref[0] = r
    buf_ref[1] = lw
    buf_ref[2] = kk
    buf_ref[3] = kk * a
    buf_ref[4] = kd
    buf_ref[5] = v

    c, nsub = RC, LCH // RC
    inc = inc_ref[...]
    sm = sm_ref[...]
    im = im_ref[...]
    stk = stk_ref[...]
    hc = B_H * c
    stack = lambda x: jnp.concatenate([x] * B_H, axis=0) * stk

    def sub(ii, carry):
        i0 = pl.multiple_of(jnp.where(d == 0, ii, nsub - 1 - ii) * c, c)
        rr = buf_ref[0, pl.ds(i0, c), :]
        lwc = buf_ref[1, pl.ds(i0, c), :]
        al = buf_ref[2, pl.ds(i0, c), :]
        be = buf_ref[3, pl.ds(i0, c), :]
        kdc = buf_ref[4, pl.ds(i0, c), :]
        vv = buf_ref[5, pl.ds(i0, c), :]
        cs = _sel_dot(inc, lwc)
        tot8 = _sel_dot(ones_ref[...], lwc)
        tot = tot8[0:1]
        e_neg = jnp.exp(-cs)
        e_end = jnp.exp(tot - cs)
        am = stack(al * jnp.exp(cs - lwc))
        rm = stack(rr * jnp.exp(cs))
        km = stack(kdc * e_neg)
        bm = stack(be * e_neg)
        kem = stack(kdc * e_end)
        bem = stack(be * e_end)
        vm = stack(vv)
        gg = _dot_nt(jnp.concatenate([am, rm], axis=0), jnp.concatenate([km, bm], axis=0))
        a_k = gg[:hc, :hc] * sm
        n_m = gg[:hc, hc:] * sm
        b_k = gg[hc:, :hc] * im
        d_m = gg[hc:, hc:] * im
        n2 = _dot3(n_m, n_m)
        n4 = _dot3(n2, n2)
        n8 = _dot3(n4, n4)
        tinv = eye_ref[...] - n_m
        tinv = tinv + _dot3(tinv, n2)
        tinv = tinv + _dot3(tinv, n4)
        tinv = tinv + _dot3(tinv, n8)
        st = st_ref[...]
        vs = jnp.dot(vm.astype(BF16), fold_ref[...], preferred_element_type=F32)
        u = _dot3(tinv, _bdot(am, st) + _bdot(a_k, vs))
        ys = _bdot(rm, st) + _bdot(b_k, vs) - _bdot(d_m, u)
        yw = _dot_sel(ys, unfold_ref[...]) * stk
        y_ref[pl.ds(i0, c), :] = _sel_dot(sel_ref[...], yw)
        gcol = jnp.exp(sum(lax.dot_general(p, e0_ref[...], (((0,), (0,)), ((), ())), preferred_element_type=F32)
                           for p in _split3(tot8)))
        st_ref[...] = gcol * st + _dot_tn(kem, vs) - _dot_tn(bem, u)
        return carry

    lax.fori_loop(0, nsub, sub, 0)


def _rwkv_call(g, zc, zr, p, layer, rc):
    blk = g.chunk_block
    rkv = lambda c: pl.BlockSpec((LCH, B_W), lambda b, d, j: (blk(b, d, j), c))
    code = lambda c: pl.BlockSpec((LCH, LANE), lambda b, d, j: (blk(b, d, j), c))
    per_dir_vec = pl.BlockSpec((None, None, 1, B_W), lambda b, d, j: (layer, d, 0, 0))
    per_dir_mat = pl.BlockSpec((None, None, LANE, B_W), lambda b, d, j: (layer, d, 0, 0))
    vec = pl.BlockSpec((None, 1, B_W), lambda b, d, j: (layer, 0, 0))
    dirc = lambda a: pl.BlockSpec((None,) + a.shape[1:], lambda b, d, j: (d,) + (0,) * (a.ndim - 1))
    const = lambda a: pl.BlockSpec(a.shape, lambda b, d, j: (0,) * a.ndim)
    out = pl.BlockSpec((None, LCH, B_W), lambda b, d, j: (d, blk(b, d, j), 0))
    return pl.pallas_call(
        _rwkv_kernel,
        out_shape=(jax.ShapeDtypeStruct((2, g.rows, B_W), F32), jax.ShapeDtypeStruct((2, g.rows, B_W), F32)),
        grid=(g.batch, 2, g.chunk_steps),
        in_specs=[rkv(ZC_BR // B_W), rkv(ZC_BK // B_W), rkv(ZC_BV // B_W),
                  code(ZR_BW // LANE), code(ZR_BA // LANE),
                  per_dir_vec, per_dir_mat, per_dir_vec, per_dir_mat, vec, vec, vec,
                  dirc(rc["inc"]), dirc(rc["sm"]), dirc(rc["im"]), const(rc["stackmask"]), const(rc["fold"]),
                  const(rc["unfold"]), const(rc["sel"]), const(rc["seg"]), const(rc["ones8"]), const(rc["e0"]),
                  const(rc["eye"])],
        out_specs=(out, out),
        scratch_shapes=[pltpu.VMEM((B_W, B_HD), F32), pltpu.VMEM((6, LCH, B_W), F32)],
        compiler_params=_cparams(("arbitrary", "arbitrary", "arbitrary")), name="rwkv_scan",
    )(zc, zc, zc, zr, zr, p["w0"], p["w_up"], p["a0"], p["a_up"], p["kk"], p["ka"], p["rk"],
      rc["inc"], rc["sm"], rc["im"], rc["stackmask"], rc["fold"], rc["unfold"], rc["sel"], rc["seg"],
      rc["ones8"], rc["e0"], rc["eye"])


def _post_kernel(ya_ref, yb_ref, bn_ref, yc_ref, ao_ref, bg_ref, co_ref, gna_ref, gnb_ref, gup_ref, gnc_ref,
                 sega_ref, segb_ref, segc_ref, o_ref):
    ha = ya_ref[0] + ya_ref[1]
    sega = sega_ref[...]
    hf = ha - _dot_sel(ha, sega) * (1.0 / A_HD)
    hn = hf * lax.rsqrt(_dot_sel(hf * hf, sega) * (1.0 / A_HD) + HEAD_EPS)
    h_a = _sigmoid(ao_ref[...]) * (hn * gna_ref[...])

    yb = yb_ref[0] + yb_ref[1]
    segb = segb_ref[...]
    yf = yb - _dot_sel(yb, segb) * (1.0 / B_HD)
    yn = yf * lax.rsqrt(_dot_sel(yf * yf, segb) * (1.0 / B_HD) + B_GN_EPS)
    ob = yn * gnb_ref[0] + gnb_ref[1] + bn_ref[0] + bn_ref[1]
    h_b = ob * _bdot(_sigmoid(bg_ref[...]), gup_ref[...])

    oc = yc_ref[0] + yc_ref[1]
    on = oc * lax.rsqrt(_dot_sel(oc * oc, segc_ref[...]) * (1.0 / C_DV) + HEAD_EPS) * gnc_ref[...]
    co = co_ref[...]
    h_c = on * (co * _sigmoid(co))
    o_ref[...] = jnp.concatenate([h_a, h_b, h_c], axis=-1).astype(BF16)


def _post_call(g, ya, yb, bonus, yc, zr, p, layer, segs):
    tm = TM_S
    two = lambda w: pl.BlockSpec((2, tm, w), lambda i: (0, i, 0))
    zcol = lambda w, off: pl.BlockSpec((tm, w), lambda i: (i, off // w))
    const = lambda a: pl.BlockSpec(a.shape, lambda i: (0,) * a.ndim)
    return pl.pallas_call(
        _post_kernel, out_shape=jax.ShapeDtypeStruct((g.rows, H_W), BF16), grid=(g.rows // tm,),
        in_specs=[two(A_W), two(B_W), two(B_W), two(C_VW),
                  zcol(A_W, ZR_AO), zcol(LANE, ZR_BG), zcol(C_VW, ZR_CO),
                  pl.BlockSpec((None, 1, A_W), lambda i: (layer, 0, 0)),
                  pl.BlockSpec((None, 2, 1, B_W), lambda i: (layer, 0, 0, 0)),
                  pl.BlockSpec((None, LANE, B_W), lambda i: (layer, 0, 0)),
                  pl.BlockSpec((None, 1, C_VW), lambda i: (layer, 0, 0)),
                  const(segs[0]), const(segs[1]), const(segs[2])],
        out_specs=pl.BlockSpec((tm, H_W), lambda i: (i, 0)),
        compiler_params=_cparams(("parallel",)), name="mixer_post",
    )(ya, yb, bonus, yc, zr, zr, zr, p["gn_a"], p["gn_b"], p["g_up"], p["gn_c"], *segs)


def _merge_kernel(h_ref, mg_ref, x_ref, gt_ref, lng_ref, lnb_ref, wcat_ref, wout_ref, o_ref, *, alpha):
    h = h_ref[...]
    mg = mg_ref[...]
    o1, o2 = A_W, A_W + B_W
    merged = (_sigmoid(mg[:, :D]) * jnp.dot(h[:, :o1], wcat_ref[:o1], preferred_element_type=F32)
              + _sigmoid(mg[:, D:2 * D]) * jnp.dot(h[:, o1:o2], wcat_ref[o1:o2], preferred_element_type=F32)
              + _sigmoid(mg[:, 2 * D:]) * jnp.dot(h[:, o2:], wcat_ref[o2:], preferred_element_type=F32))
    y = jnp.dot(merged.astype(BF16), wout_ref[...], preferred_element_type=F32)
    o_ref[...] = _ln(alpha * x_ref[...] + gt_ref[...] * y) * lng_ref[...] + lnb_ref[...]


def _ln_spec(layer, sub, which):
    return pl.BlockSpec((None, None, None, 1, D), lambda i, *_: (layer, sub, which, 0, 0))


def _merge_call(g, h, zr, x, modflat, ln_post, wcat, wout, layer, alpha):
    tm = TM_S
    return pl.pallas_call(
        functools.partial(_merge_kernel, alpha=alpha), out_shape=jax.ShapeDtypeStruct((g.rows, D), F32),
        grid=(g.rows // tm,),
        in_specs=[pl.BlockSpec((tm, H_W), lambda i: (i, 0)),
                  pl.BlockSpec((tm, 3 * D), lambda i: (i, 0)),
                  pl.BlockSpec((tm, D), lambda i: (i, 0)),
                  g.mod_spec(layer, 2, tm), _ln_spec(layer, 0, 0), _ln_spec(layer, 0, 1),
                  pl.BlockSpec((None, H_W, D), lambda i: (layer, 0, 0), pipeline_mode=pl.Buffered(1)),
                  pl.BlockSpec((None, D, D), lambda i: (layer, 0, 0), pipeline_mode=pl.Buffered(1))],
        out_specs=pl.BlockSpec((tm, D), lambda i: (i, 0)),
        compiler_params=_cparams(("parallel",)), name="branch_merge",
    )(h, zr, x, modflat, ln_post, ln_post, wcat, wout)


def _ffn_kernel(x_ref, sh_ref, sc_ref, gt_ref, lng_ref, lnb_ref, wg_ref, wu_ref, wd_ref, o_ref, u_ref, acc_ref,
                *, alpha):
    j = pl.program_id(1)

    @pl.when(j == 0)
    def _():
        u_ref[...] = (_ln(x_ref[...]) * (1.0 + sc_ref[...]) + sh_ref[...]).astype(BF16)
        acc_ref[...] = jnp.zeros_like(acc_ref)

    u = u_ref[...]
    gate = jnp.dot(u, wg_ref[...], preferred_element_type=F32)
    up = jnp.dot(u, wu_ref[...], preferred_element_type=F32)
    hid = (gate * _sigmoid(gate)) * up
    acc_ref[...] += jnp.dot(hid.astype(BF16), wd_ref[...], preferred_element_type=F32)

    @pl.when(j == pl.num_programs(1) - 1)
    def _():
        o_ref[...] = _ln(alpha * x_ref[...] + gt_ref[...] * acc_ref[...]) * lng_ref[...] + lnb_ref[...]


def _ffn_call(g, x, modflat, ln_post, wg, wu, wd, layer, idx, alpha):
    tf = 512
    return pl.pallas_call(
        functools.partial(_ffn_kernel, alpha=alpha), out_shape=jax.ShapeDtypeStruct((g.rows, D), F32),
        grid=(g.rows // TM, D_FFP // tf),
        in_specs=[pl.BlockSpec((TM, D), lambda i, j: (i, 0)),
                  g.mod_spec(layer, 3, TM), g.mod_spec(layer, 4, TM), g.mod_spec(layer, 5, TM),
                  _ln_spec(layer, 1, 0), _ln_spec(layer, 1, 1),
                  pl.BlockSpec((None, D, tf), lambda i, j: (idx, 0, j)),
                  pl.BlockSpec((None, D, tf), lambda i, j: (idx, 0, j)),
                  pl.BlockSpec((None, tf, D), lambda i, j: (idx, j, 0))],
        out_specs=pl.BlockSpec((TM, D), lambda i, j: (i, 0)),
        scratch_shapes=[pltpu.VMEM((TM, D), BF16), pltpu.VMEM((TM, D), F32)],
        compiler_params=_cparams(("parallel", "arbitrary")), name="ffn_swiglu",
    )(x, modflat, modflat, modflat, ln_post, ln_post, wg, wu, wd)


def _router_kernel(x_ref, sh_ref, sc_ref, w_ref, b_ref, u_ref, comb_ref):
    u = _ln(x_ref[...]) * (1.0 + sc_ref[...]) + sh_ref[...]
    u_ref[...] = u.astype(BF16)
    lane = lax.broadcasted_iota(jnp.int32, (1, LANE), 1)
    logits = jnp.where(lane < N_EXPERTS, _dot3(u, w_ref[...]) + b_ref[...], -jnp.inf)
    m1 = jnp.max(logits, axis=-1, keepdims=True)
    i1 = jnp.min(jnp.where(logits == m1, lane, LANE), axis=-1, keepdims=True)
    rest = jnp.where(lane == i1, -jnp.inf, logits)
    m2 = jnp.max(rest, axis=-1, keepdims=True)
    i2 = jnp.min(jnp.where(rest == m2, lane, LANE), axis=-1, keepdims=True)
    e2 = jnp.exp(m2 - m1)
    p1 = 1.0 / (1.0 + e2)
    comb_ref[...] = jnp.where(lane == i1, p1, 0.0) + jnp.where(lane == i2, e2 * p1, 0.0)


def _router_call(g, x, modflat, router, router_b, layer, idx):
    return pl.pallas_call(
        _router_kernel,
        out_shape=(jax.ShapeDtypeStruct((g.rows, D), BF16), jax.ShapeDtypeStruct((g.rows, LANE), F32)),
        grid=(g.rows // TM,),
        in_specs=[pl.BlockSpec((TM, D), lambda i: (i, 0)), g.mod_spec(layer, 3, TM), g.mod_spec(layer, 4, TM),
                  pl.BlockSpec((None, D, LANE), lambda i: (idx, 0, 0)),
                  pl.BlockSpec((None, 1, LANE), lambda i: (idx, 0, 0))],
        out_specs=(pl.BlockSpec((TM, D), lambda i: (i, 0)), pl.BlockSpec((TM, LANE), lambda i: (i, 0))),
        compiler_params=_cparams(("parallel",)), name="moe_router",
    )(x, modflat, modflat, router, router_b)


def _moe_kernel(u_ref, comb_ref, x_ref, gt_ref, lng_ref, lnb_ref, wg_ref, wu_ref, wd_ref, o_ref, acc_ref, *, alpha):
    e = pl.program_id(1)
    j = pl.program_id(2)

    @pl.when(jnp.logical_and(e == 0, j == 0))
    def _():
        acc_ref[...] = jnp.zeros_like(acc_ref)

    u = u_ref[...]
    lane = lax.broadcasted_iota(jnp.int32, (1, LANE), 1)
    ce = jnp.sum(jnp.where(lane == e, comb_ref[...], 0.0), axis=-1, keepdims=True)
    gate = jnp.dot(u, wg_ref[...], preferred_element_type=F32)
    up = jnp.dot(u, wu_ref[...], preferred_element_type=F32)
    hid = (gate * _sigmoid(gate)) * up
    acc_ref[...] += ce * jnp.dot(hid.astype(BF16), wd_ref[...], preferred_element_type=F32)

    @pl.when(jnp.logical_and(e == pl.num_programs(1) - 1, j == pl.num_programs(2) - 1))
    def _():
        o_ref[...] = _ln(alpha * x_ref[...] + gt_ref[...] * acc_ref[...]) * lng_ref[...] + lnb_ref[...]


def _moe_call(g, u, comb, x, modflat, ln_post, wg, wu, wd, layer, idx, alpha):
    tf = 512
    return pl.pallas_call(
        functools.partial(_moe_kernel, alpha=alpha), out_shape=jax.ShapeDtypeStruct((g.rows, D), F32),
        grid=(g.rows // TM, N_EXPERTS, D_FFP // tf),
        in_specs=[pl.BlockSpec((TM, D), lambda i, e, j: (i, 0)),
                  pl.BlockSpec((TM, LANE), lambda i, e, j: (i, 0)),
                  pl.BlockSpec((TM, D), lambda i, e, j: (i, 0)),
                  g.mod_spec(layer, 5, TM), _ln_spec(layer, 1, 0), _ln_spec(layer, 1, 1),
                  pl.BlockSpec((None, None, D, tf), lambda i, e, j: (idx, e, 0, j)),
                  pl.BlockSpec((None, None, D, tf), lambda i, e, j: (idx, e, 0, j)),
                  pl.BlockSpec((None, None, tf, D), lambda i, e, j: (idx, e, j, 0))],
        out_specs=pl.BlockSpec((TM, D), lambda i, e, j: (i, 0)),
        scratch_shapes=[pltpu.VMEM((TM, D), F32)],
        compiler_params=_cparams(("parallel", "arbitrary", "arbitrary")), name="moe_experts",
    )(u, comb, x, modflat, ln_post, ln_post, wg, wu, wd)


def kernel(x, c, ctx, c_ctx, w_mod, b_mod, w_in, conv_k, mlstm_gate_b, mlstm_gn, rwkv_w0, rwkv_w_up, rwkv_a0,
           rwkv_a_up, rwkv_g_up, rwkv_kk, rwkv_ka, rwkv_rk, rwkv_gn, gla_gate_up, gla_gate_b, gla_gn, w_branch_a,
           w_branch_b, w_branch_c, w_out, ln_post, ffn_w_gate, ffn_w_up, ffn_w_down, moe_router, moe_router_b,
           moe_w_gate, moe_w_up, moe_w_down):
    batch, seq, _ = x.shape
    ctx_len = ctx.shape[1]
    depth = w_mod.shape[0]
    alpha = (2.0 * depth) ** 0.25
    g = _Geom(batch, seq, ctx_len)

    n_conv = 5504
    w_conv = _pack_conv_cols(w_in[..., :n_conv]).astype(BF16)
    w_rest = _pack_rest_cols(w_in[..., n_conv:]).astype(BF16)
    conv_w = _pack_conv_cols(conv_k.reshape(depth, 9, n_conv))
    gate_b = _pad_last(mlstm_gate_b.reshape(depth, 1, 16), LANE)

    def rows_at(a, per, n):
        out = jnp.zeros((depth, 2, n, a.shape[-1]), a.dtype)
        for d in range(2):
            out = out.at[:, d, d * per:(d + 1) * per].set(a[:, d])
        return out

    rw = dict(w0=rwkv_w0.reshape(depth, 2, 1, B_W), w_up=rows_at(rwkv_w_up, B_W_LORA, LANE).astype(BF16),
              a0=rwkv_a0.reshape(depth, 2, 1, B_W), a_up=rows_at(rwkv_a_up, B_A_LORA, LANE).astype(BF16),
              kk=rwkv_kk.reshape(depth, 1, B_W), ka=rwkv_ka.reshape(depth, 1, B_W),
              rk=rwkv_rk.reshape(depth, 1, B_W))
    gla_up = _pad_last(rows_at(gla_gate_up, C_G_LORA, LANE), C_KP).astype(BF16)
    gla_b = _pad_last(gla_gate_b.reshape(depth, 2, 1, C_KW), C_KP)
    post_p = dict(gn_a=_pad_heads(mlstm_gn, A_H, A_HD, A_HP).reshape(depth, 1, A_W),
                  gn_b=rwkv_gn.reshape(depth, 2, 1, B_W), g_up=rwkv_g_up.astype(BF16),
                  gn_c=gla_gn.reshape(depth, 1, C_VW))
    wa = _pad_heads(jnp.swapaxes(w_branch_a, 1, 2), A_H, A_HD, A_HP)
    wcat = jnp.concatenate([jnp.swapaxes(wa, 1, 2), w_branch_b, w_branch_c], axis=1).astype(BF16)
    wout = w_out.astype(BF16)
    lnp = ln_post.reshape(depth, 2, 2, 1, D)
    padf = lambda a: _pad_last(a, D_FFP).astype(BF16)
    padr = lambda a: jnp.pad(a, [(0, 0)] * (a.ndim - 2) + [(0, D_FFP - D_FF), (0, 0)]).astype(BF16)
    f_wg, f_wu, f_wd = padf(ffn_w_gate), padf(ffn_w_up), padr(ffn_w_down)
    m_wg, m_wu, m_wd = padf(moe_w_gate), padf(moe_w_up), padr(moe_w_down)
    router = _pad_last(moe_router, LANE)
    router_b = _pad_last(moe_router_b.reshape(-1, 1, N_EXPERTS), LANE)

    inc_a = _mlstm_consts()
    gla_c = _gla_consts()
    rwkv_c = _rwkv_consts()
    segs = _seg_consts()

    cvec = jnp.concatenate([c, c_ctx[None, :], jnp.zeros((8 - batch - 1, D), F32)], axis=0)
    modflat = _mod_call(cvec, w_mod, b_mod).reshape(depth * 8, 1, 6 * D)

    xs = jnp.concatenate([x.reshape(batch * seq, D), ctx.reshape(batch * ctx_len, D)], axis=0)
    for l in range(depth):
        zc = _conv_call(g, _inproj_call(g, xs, modflat, l, w_conv, 2176, "in_proj_conv"), conv_w, l)
        zr = _inproj_call(g, xs, modflat, l, w_rest, 1408, "in_proj_rest")
        ya = _mlstm_call(g, zc, zr, gate_b, l, inc_a)
        yb, bonus = _rwkv_call(g, zc, zr, rw, l, rwkv_c)
        yc = _gla_call(g, zc, zr, gla_up, gla_b, l, gla_c)
        h = _post_call(g, ya, yb, bonus, yc, zr, post_p, l, segs)
        xs = _merge_call(g, h, zr, xs, modflat, lnp, wcat, wout, l, alpha)
        i = l // 2
        if l % 2 == 0:
            xs = _ffn_call(g, xs, modflat, lnp, f_wg, f_wu, f_wd, l, i, alpha)
        else:
            u, comb = _router_call(g, xs, modflat, router, router_b, l, i)
            xs = _moe_call(g, u, comb, xs, modflat, lnp, m_wg, m_wu, m_wd, l, i, alpha)
    return xs[:batch * seq].reshape(batch, seq, D)
```

```python
import functools
import math

import jax
import jax.numpy as jnp
import numpy as np
from jax import lax
from jax.experimental import pallas as pl
from jax.experimental.pallas import tpu as pltpu

F32 = jnp.float32
BF16 = jnp.bfloat16

D = 2048
GRID_W = 64
A_H, A_HD, A_HP = 4, 192, 256
A_W = A_H * A_HP
B_H, B_HD = 10, 64
B_W = B_H * B_HD
B_W_LORA, B_A_LORA, B_G_LORA = 64, 32, 128
B_GN_EPS = 64e-5
C_H, C_DK, C_DV = 4, 80, 160
C_KW, C_KP, C_VW = C_H * C_DK, 384, C_H * C_DV
C_G_LORA = 16
C_TAU = 16.0
D_FF, D_FFP = 5504, 5632
N_EXPERTS = 8
LN_EPS = 1e-6
HEAD_EPS = 1e-5
LANE = 128

ZC_AQ, ZC_AK, ZC_AV = 0, 1024, 2048
ZC_BR, ZC_BK, ZC_BV, ZC_CV = 3200, 3840, 4480, 5120
ZC_CQ, ZC_CK = 5760, 6144
ZC_W = 6528
ZR_MG, ZR_AO, ZR_AG, ZR_BW, ZR_BA, ZR_BG, ZR_CO, ZR_CG = 0, 6144, 7168, 7296, 7424, 7552, 7680, 8320
ZR_W = 8448
H_W = A_W + B_W + C_VW

TM = 512
TM_S = 256
LCH = 128
RC = 64
VMEM_LIMIT = 56 * 1024 * 1024


def _cparams(sem):
    return pltpu.CompilerParams(dimension_semantics=sem, vmem_limit_bytes=VMEM_LIMIT)


def _ln(a):
    mu = jnp.mean(a, axis=-1, keepdims=True)
    d = a - mu
    var = jnp.mean(d * d, axis=-1, keepdims=True)
    return d * lax.rsqrt(var + LN_EPS)


def _sigmoid(x):
    return 1.0 / (1.0 + jnp.exp(-x))


def _softplus(x):
    return jnp.maximum(x, 0.0) + jnp.log(1.0 + jnp.exp(-jnp.abs(x)))


def _bdot(a, b):
    return jnp.dot(a.astype(BF16), b.astype(BF16), preferred_element_type=F32)


def _dot_nt(a, b):
    return lax.dot_general(a.astype(BF16), b.astype(BF16), (((1,), (1,)), ((), ())), preferred_element_type=F32)


def _dot_tn(a, b):
    return lax.dot_general(a.astype(BF16), b.astype(BF16), (((0,), (0,)), ((), ())), preferred_element_type=F32)


def _split2(x):
    x1 = x.astype(BF16)
    return x1, (x - x1.astype(F32)).astype(BF16)


def _split3(x):
    x1 = x.astype(BF16)
    r = x - x1.astype(F32)
    x2 = r.astype(BF16)
    return x1, x2, (r - x2.astype(F32)).astype(BF16)


def _sel_dot(m, x):
    return sum(jnp.dot(m, p, preferred_element_type=F32) for p in _split3(x))


def _dot_sel(x, m):
    return sum(jnp.dot(p, m, preferred_element_type=F32) for p in _split2(x))


def _dot3(a, b):
    a1, a2 = _split2(a)
    b1, b2 = _split2(b)
    return (jnp.dot(a1, b1, preferred_element_type=F32) + jnp.dot(a1, b2, preferred_element_type=F32)
            + jnp.dot(a2, b1, preferred_element_type=F32))


def _dir_pos(n):
    return np.stack([np.arange(n), n - 1 - np.arange(n)])


def _mlstm_consts():
    pos = _dir_pos(LCH)
    inc = (pos[:, None, :] <= pos[:, :, None]).astype(np.float32)
    return jnp.asarray(inc, BF16)


def _gla_consts():
    L = LCH
    pos = _dir_pos(L)
    widths = [L >> i for i in range(int(math.log2(L)))]
    nl = len(widths)
    wmat = np.zeros((2, (2 * nl + 2) * L + 8, L), np.float32)
    masks = np.zeros((2, (nl + 1) * C_H * L, L), np.float32)
    for d in range(2):
        p = pos[d]
        for li, w in enumerate(widths):
            half = w // 2
            blk, off = p // w, p % w
            mid = blk * w + half
            isq, isk = off >= half, off < half
            eq = (p[None, :] >= mid[:, None]) & (p[None, :] <= p[:, None]) & isq[:, None]
            ek = (p[None, :] > p[:, None]) & (p[None, :] <= mid[:, None] - 1) & isk[:, None]
            wmat[d, li * L:(li + 1) * L] = eq
            wmat[d, (nl + li) * L:(nl + li + 1) * L] = ek
            m = (blk[:, None] == blk[None, :]) & isq[:, None] & isk[None, :]
            masks[d, li * C_H * L:(li + 1) * C_H * L] = np.tile(m, (C_H, 1))
        wmat[d, 2 * nl * L:(2 * nl + 1) * L] = p[None, :] <= p[:, None]
        wmat[d, (2 * nl + 1) * L:(2 * nl + 2) * L] = p[None, :] > p[:, None]
        wmat[d, (2 * nl + 2) * L:] = 1.0
        masks[d, nl * C_H * L:] = np.tile(np.eye(L), (C_H, 1))
    qmask = np.zeros((C_H, 1, C_KP), np.float32)
    vmask = np.zeros((C_H, 1, C_VW), np.float32)
    bd = np.zeros((C_VW, C_KP), np.float32)
    for h in range(C_H):
        qmask[h, 0, h * C_DK:(h + 1) * C_DK] = 1.0
        vmask[h, 0, h * C_DV:(h + 1) * C_DV] = 1.0
        bd[h * C_DV:(h + 1) * C_DV, h * C_DK:(h + 1) * C_DK] = 1.0
    return (jnp.asarray(wmat, BF16), jnp.asarray(masks, F32), jnp.asarray(qmask, F32), jnp.asarray(vmask, F32),
            jnp.asarray(bd, F32), nl)


def _rwkv_consts():
    c, H, L = RC, B_H, LCH
    pos = _dir_pos(c)
    inc = (pos[:, None, :] <= pos[:, :, None]).astype(np.float32)
    nsub = L // c
    incbd = np.zeros((2, L, L), np.float32)
    for d in range(2):
        for i in range(nsub):
            incbd[d, i * c:(i + 1) * c, i * c:(i + 1) * c] = inc[d]
    sub = np.arange(L) // c
    same = sub[:, None] == sub[None, :]
    totbd = same.astype(np.float32)
    pp = np.stack([np.tile(pos[d], nsub) for d in range(2)])
    sm = (same[None] & (pp[:, None, :] < pp[:, :, None])).astype(np.float32)
    im = (same[None] & (pp[:, None, :] <= pp[:, :, None])).astype(np.float32)
    lane_head = np.repeat(np.arange(H), B_HD)
    seg = (lane_head[:, None] == lane_head[None, :]).astype(np.float32)
    return dict(incbd=jnp.asarray(incbd, BF16), totbd=jnp.asarray(totbd, BF16), sm=jnp.asarray(sm, F32),
                im=jnp.asarray(im, F32), seg=jnp.asarray(seg, BF16), eye=jnp.eye(L, dtype=F32))


def _seg_consts():
    la = np.arange(A_W)
    va = (la % A_HP) < A_HD
    sega = ((la[:, None] // A_HP) == (la[None, :] // A_HP)) & va[:, None] & va[None, :]
    lb = np.arange(B_W) // B_HD
    segb = lb[:, None] == lb[None, :]
    lc = np.arange(C_VW) // C_DV
    segc = lc[:, None] == lc[None, :]
    return jnp.asarray(sega, BF16), jnp.asarray(segb, BF16), jnp.asarray(segc, BF16)


def _pad_last(a, n):
    return jnp.pad(a, [(0, 0)] * (a.ndim - 1) + [(0, n - a.shape[-1])])


def _pad_heads(a, nh, hd, hp):
    a = a.reshape(a.shape[:-1] + (nh, hd))
    a = jnp.pad(a, [(0, 0)] * (a.ndim - 1) + [(0, hp - hd)])
    return a.reshape(a.shape[:-2] + (nh * hp,))


def _pack_conv_cols(w):
    s = lambda a, b: w[..., a:b]
    z = lambda n: jnp.zeros(w.shape[:-1] + (n,), w.dtype)
    parts = [_pad_heads(s(0, 768), A_H, A_HD, A_HP), _pad_heads(s(768, 1536), A_H, A_HD, A_HP),
             _pad_heads(s(1536, 2304), A_H, A_HD, A_HP), z(128),
             s(2304, 2944), s(2944, 3584), s(3584, 4224), s(4864, 5504),
             _pad_last(s(4224, 4544), C_KP), _pad_last(s(4544, 4864), C_KP)]
    return jnp.concatenate(parts, axis=-1)


def _pack_rest_cols(w):
    s = lambda a, b: w[..., a:b]
    parts = [s(0, 6144), _pad_heads(s(6144, 6912), A_H, A_HD, A_HP), _pad_last(s(6912, 6928), LANE),
             s(6928, 7056), _pad_last(s(7056, 7120), LANE), s(7120, 7248), s(7280, 7920),
             _pad_last(s(7248, 7280), LANE)]
    return jnp.concatenate(parts, axis=-1)


def _mod_kernel(c_ref, w_ref, b_ref, o_ref):
    c = c_ref[...]
    s = c * _sigmoid(c)
    o_ref[...] = jnp.dot(s, w_ref[...], preferred_element_type=F32) + b_ref[...]


def _mod_call(cvec, w_mod, b_mod):
    depth, _, n6 = w_mod.shape
    tn = 1536
    return pl.pallas_call(
        _mod_kernel, out_shape=jax.ShapeDtypeStruct((depth, 8, n6), F32), grid=(depth, n6 // tn),
        in_specs=[pl.BlockSpec((8, D), lambda l, j: (0, 0)),
                  pl.BlockSpec((None, D, tn), lambda l, j: (l, 0, j)),
                  pl.BlockSpec((None, 1, tn), lambda l, j: (l, 0, j))],
        out_specs=pl.BlockSpec((None, 8, tn), lambda l, j: (l, 0, j)),
        compiler_params=_cparams(("parallel", "parallel")), name="mod_vectors",
    )(cvec, w_mod, b_mod.reshape(depth, 1, n6))


class _Geom:
    def __init__(self, batch, seq, ctx_len):
        assert seq % TM == 0 and (batch * ctx_len) % TM == 0 and TM % ctx_len == 0 and TM % GRID_W == 0
        assert ctx_len % LCH == 0 and seq % LCH == 0 and ctx_len & (ctx_len - 1) == 0
        self.batch, self.seq, self.ctx_len = batch, seq, ctx_len
        self.rows = batch * (seq + ctx_len)
        self.n_lat_rows = batch * seq

    def mod_row(self, i, tm):
        return jnp.minimum(i // (self.seq // tm), self.batch)

    def mod_spec(self, layer, k, tm):
        return pl.BlockSpec((None, 1, D), lambda i, *_: (layer * 8 + self.mod_row(i, tm), 0, k))

    def chunk_block(self, b, d, j):
        ncc, nlc = self.ctx_len // LCH, self.seq // LCH
        is_ctx = j < ncc
        jj = jnp.where(is_ctx, j, j - ncc)
        n = jnp.where(is_ctx, ncc, nlc)
        loc = jnp.where(d == 0, jj, n - 1 - jj)
        return jnp.where(is_ctx, self.n_lat_rows // LCH + b * ncc + loc, b * nlc + loc)

    @property
    def chunk_steps(self):
        return (self.ctx_len + self.seq) // LCH


def _inproj_kernel(x_ref, sh_ref, sc_ref, w_ref, o_ref, u_ref):
    @pl.when(pl.program_id(1) == 0)
    def _():
        u_ref[...] = (_ln(x_ref[...]) * (1.0 + sc_ref[...]) + sh_ref[...]).astype(BF16)

    o_ref[...] = jnp.dot(u_ref[...], w_ref[...], preferred_element_type=F32)


def _inproj_call(g, x, modflat, layer, w, tn, name):
    n = w.shape[-1]
    return pl.pallas_call(
        _inproj_kernel, out_shape=jax.ShapeDtypeStruct((g.rows, n), F32), grid=(g.rows // TM, n // tn),
        in_specs=[pl.BlockSpec((TM, D), lambda i, j: (i, 0)),
                  g.mod_spec(layer, 0, TM), g.mod_spec(layer, 1, TM),
                  pl.BlockSpec((None, D, tn), lambda i, j: (layer, 0, j))],
        out_specs=pl.BlockSpec((TM, tn), lambda i, j: (i, j)),
        scratch_shapes=[pltpu.VMEM((TM, D), BF16)],
        compiler_params=_cparams(("parallel", "arbitrary")), name=name,
    )(x, modflat, modflat, w)


def _conv_kernel(prev_ref, main_ref, next_ref, w_ref, o_ref, *, n_lat_tiles, tiles_per_batch, ctx_len):
    i = pl.program_id(0)
    is_ctx = i >= n_lat_tiles
    ib = i % tiles_per_batch
    top_ok = jnp.logical_and(ib != 0, jnp.logical_not(is_ctx))
    bot_ok = jnp.logical_and(ib != tiles_per_batch - 1, jnp.logical_not(is_ctx))
    t = lax.broadcasted_iota(jnp.int32, (TM, 1), 0)
    main = main_ref[...]
    up = jnp.concatenate([prev_ref[...], main[:TM - GRID_W]], axis=0)
    dn = jnp.concatenate([main[GRID_W:], next_ref[...]], axis=0)
    lat = jnp.logical_not(is_ctx)
    up = jnp.where(jnp.logical_and(lat, jnp.logical_or(t >= GRID_W, top_ok)), up, 0.0)
    dn = jnp.where(jnp.logical_and(lat, jnp.logical_or(t < TM - GRID_W, bot_ok)), dn, 0.0)
    w = w_ref[...]
    a0 = w[0:1] * up + w[3:4] * main + w[6:7] * dn
    a1 = w[1:2] * up + w[4:5] * main + w[7:8] * dn
    a2 = w[2:3] * up + w[5:6] * main + w[8:9] * dn
    col = jnp.where(is_ctx, t & (ctx_len - 1), t & (GRID_W - 1))
    last = jnp.where(is_ctx, ctx_len - 1, GRID_W - 1)
    o_ref[...] = (a1 + jnp.where(col != 0, pltpu.roll(a0, 1, 0), 0.0)
                  + jnp.where(col != last, pltpu.roll(a2, TM - 1, 0), 0.0))


def _conv_call(g, z, conv_w, layer):
    tc = 2176
    hb = TM // GRID_W
    nb64 = g.rows // GRID_W
    kern = functools.partial(_conv_kernel, n_lat_tiles=g.n_lat_rows // TM, tiles_per_batch=g.seq // TM,
                             ctx_len=g.ctx_len)
    return pl.pallas_call(
        kern, out_shape=jax.ShapeDtypeStruct((g.rows, ZC_W), F32), grid=(g.rows // TM, ZC_W // tc),
        in_specs=[pl.BlockSpec((GRID_W, tc), lambda i, j: (jnp.maximum(i * hb - 1, 0), j)),
                  pl.BlockSpec((TM, tc), lambda i, j: (i, j)),
                  pl.BlockSpec((GRID_W, tc), lambda i, j: (jnp.minimum(i * hb + hb, nb64 - 1), j)),
                  pl.BlockSpec((None, 9, tc), lambda i, j: (layer, 0, j))],
        out_specs=pl.BlockSpec((TM, tc), lambda i, j: (i, j)),
        compiler_params=_cparams(("parallel", "parallel")), name="grid_conv",
    )(z, z, z, conv_w)


def _mlstm_kernel(q_ref, k_ref, v_ref, g_ref, gb_ref, inc_ref, eye_ref, y_ref, c_ref, n_ref, m_ref):
    d = pl.program_id(1)
    j = pl.program_id(2)

    @pl.when(j == 0)
    def _():
        c_ref[...] = jnp.zeros_like(c_ref)
        n_ref[...] = jnp.zeros_like(n_ref)
        m_ref[...] = jnp.zeros_like(m_ref)

    L = LCH
    inc = inc_ref[...]
    g = g_ref[...] + gb_ref[...]
    lf = -_softplus(-g)
    b_all = _sel_dot(inc, lf)
    r_all = g - pltpu.roll(b_all, LANE - A_H, 1)
    rt_all = sum(lax.dot_general(eye_ref[...], p, (((1,), (1,)), ((), ())), preferred_element_type=F32)
                 for p in _split3(r_all))
    is_f = d == 0
    valid = inc > 0
    lane = lax.broadcasted_iota(jnp.int32, (1, LANE), 1)
    m_all = m_ref[...]
    outs = []
    for h in range(A_H):
        li = jnp.where(is_f, g[:, h:h + 1], g[:, 8 + h:9 + h])
        b = jnp.where(is_f, b_all[:, 4 + h:5 + h], b_all[:, 12 + h:13 + h])
        r_row = jnp.where(is_f, rt_all[h:h + 1, :], rt_all[8 + h:9 + h, :])
        b_last = jnp.where(is_f, b_all[L - 1:L, 4 + h:5 + h], b_all[0:1, 12 + h:13 + h])
        m_prev = m_all[:, h:h + 1]
        log_d = jnp.where(valid, b + r_row, -jnp.inf)
        m_inter = b + m_prev
        m_t = jnp.maximum(m_inter, jnp.max(log_d, axis=-1, keepdims=True))
        qh = q_ref[:, h * A_HP:(h + 1) * A_HP]
        kh = k_ref[:, h * A_HP:(h + 1) * A_HP] * (A_HD ** -0.5)
        vh = v_ref[:, h * A_HP:(h + 1) * A_HP]
        s = _dot_nt(qh, kh) * jnp.exp(log_d - m_t)
        w_inter = jnp.exp(m_inter - m_t)
        num = _bdot(s, vh) + w_inter * _bdot(qh, c_ref[h])
        den = jnp.sum(s, axis=-1, keepdims=True) + w_inter * jnp.sum(qh * n_ref[h], axis=-1, keepdims=True)
        outs.append(num / jnp.maximum(jnp.abs(den), jnp.exp(-m_t)))
        log_w = b_last - b + li
        m_new = jnp.maximum(b_last + m_prev, jnp.max(log_w, axis=0, keepdims=True))
        kw = kh * jnp.exp(log_w - m_new)
        cd = jnp.exp(b_last + m_prev - m_new)
        c_ref[h] = cd * c_ref[h] + _dot_tn(kw, vh)
        n_ref[h] = cd * n_ref[h] + jnp.sum(kw, axis=0, keepdims=True)
        m_all = jnp.where(lane == h, m_new, m_all)
    m_ref[...] = m_all
    y_ref[...] = jnp.concatenate(outs, axis=-1)


def _mlstm_call(g, zc, zr, gate_b, layer, inc):
    blk = g.chunk_block
    qkv = lambda c: pl.BlockSpec((LCH, A_W), lambda b, d, j: (blk(b, d, j), c))
    return pl.pallas_call(
        _mlstm_kernel, out_shape=jax.ShapeDtypeStruct((2, g.rows, A_W), F32),
        grid=(g.batch, 2, g.chunk_steps),
        in_specs=[qkv(ZC_AQ // A_W), qkv(ZC_AK // A_W), qkv(ZC_AV // A_W),
                  pl.BlockSpec((LCH, LANE), lambda b, d, j: (blk(b, d, j), ZR_AG // LANE)),
                  pl.BlockSpec((None, 1, LANE), lambda b, d, j: (layer, 0, 0)),
                  pl.BlockSpec((None, LCH, LCH), lambda b, d, j: (d, 0, 0)),
                  pl.BlockSpec((LANE, LANE), lambda b, d, j: (0, 0))],
        out_specs=pl.BlockSpec((None, LCH, A_W), lambda b, d, j: (d, blk(b, d, j), 0)),
        scratch_shapes=[pltpu.VMEM((A_H, A_HP, A_HP), F32), pltpu.VMEM((A_H, 1, A_HP), F32),
                        pltpu.VMEM((1, LANE), F32)],
        compiler_params=_cparams(("arbitrary", "arbitrary", "arbitrary")), name="mlstm_scan",
    )(zc, zc, zc, zr, gate_b, inc, jnp.eye(LANE, dtype=BF16))


def _gla_kernel(q_ref, k_ref, v_ref, code_ref, gup_ref, gb_ref, w_ref, mask_ref, qm_ref, vm_ref, bd_ref,
                y_ref, st_ref, *, nl):
    j = pl.program_id(2)

    @pl.when(j == 0)
    def _():
        st_ref[...] = jnp.zeros_like(st_ref)

    L = LCH
    x = _bdot(code_ref[...], gup_ref[...]) + gb_ref[...]
    glog = -_softplus(-x) * (1.0 / C_TAU)
    e = _sel_dot(w_ref[...], glog)
    q = q_ref[...] * (C_DK ** -0.5)
    k = k_ref[...]
    v = v_ref[...]
    stack = lambda a: jnp.concatenate([a * qm_ref[h] for h in range(C_H)], axis=0)
    att = jnp.zeros((C_H * L, L), F32)
    for lv in range(nl):
        qs = q * jnp.exp(e[lv * L:(lv + 1) * L])
        ks = k * jnp.exp(e[(nl + lv) * L:(nl + lv + 1) * L])
        att = att + mask_ref[lv * C_H * L:(lv + 1) * C_H * L, :] * _dot_nt(stack(qs), ks)
    att = att + mask_ref[nl * C_H * L:(nl + 1) * C_H * L, :] * _dot_nt(stack(q), k)
    o = _dot_nt(q * jnp.exp(e[2 * nl * L:(2 * nl + 1) * L]), st_ref[...])
    for h in range(C_H):
        o = o + vm_ref[h] * _bdot(att[h * L:(h + 1) * L], v)
    y_ref[...] = o
    ke = k * jnp.exp(e[(2 * nl + 1) * L:(2 * nl + 2) * L])
    tot = e[(2 * nl + 2) * L:(2 * nl + 2) * L + 1]
    st_ref[...] = st_ref[...] * jnp.exp(tot) + bd_ref[...] * _dot_tn(v, ke)


def _gla_call(g, zc, zr, gate_up, gate_b, layer, consts):
    wmat, masks, qmask, vmask, bd, nl = consts
    blk = g.chunk_block
    const = lambda a: pl.BlockSpec(a.shape, lambda b, d, j: (0,) * a.ndim)
    return pl.pallas_call(
        functools.partial(_gla_kernel, nl=nl), out_shape=jax.ShapeDtypeStruct((2, g.rows, C_VW), F32),
        grid=(g.batch, 2, g.chunk_steps),
        in_specs=[pl.BlockSpec((LCH, C_KP), lambda b, d, j: (blk(b, d, j), ZC_CQ // C_KP)),
                  pl.BlockSpec((LCH, C_KP), lambda b, d, j: (blk(b, d, j), ZC_CK // C_KP)),
                  pl.BlockSpec((LCH, C_VW), lambda b, d, j: (blk(b, d, j), ZC_CV // C_VW)),
                  pl.BlockSpec((LCH, LANE), lambda b, d, j: (blk(b, d, j), ZR_CG // LANE)),
                  pl.BlockSpec((None, None, LANE, C_KP), lambda b, d, j: (layer, d, 0, 0)),
                  pl.BlockSpec((None, None, 1, C_KP), lambda b, d, j: (layer, d, 0, 0)),
                  pl.BlockSpec((None,) + wmat.shape[1:], lambda b, d, j: (d, 0, 0)),
                  pl.BlockSpec((None,) + masks.shape[1:], lambda b, d, j: (d, 0, 0)),
                  const(qmask), const(vmask), const(bd)],
        out_specs=pl.BlockSpec((None, LCH, C_VW), lambda b, d, j: (d, blk(b, d, j), 0)),
        scratch_shapes=[pltpu.VMEM((C_VW, C_KP), F32)],
        compiler_params=_cparams(("arbitrary", "arbitrary", "arbitrary")), name="gla_scan",
    )(zc, zc, zc, zr, gate_up, gate_b, wmat, masks, qmask, vmask, bd)


def _rwkv_kernel(r_ref, k_ref, v_ref, wc_ref, ac_ref, w0_ref, wup_ref, a0_ref, aup_ref, kk_ref, ka_ref, rk_ref,
                 incbd_ref, totbd_ref, sm_ref, im_ref, seg_ref, eye_ref, y_ref, bonus_ref,
                 st_ref, lhs_ref, bes_ref, pt_ref, y1_ref, gam_ref):
    d = pl.program_id(1)
    j = pl.program_id(2)

    @pl.when(j == 0)
    def _():
        st_ref[...] = jnp.zeros_like(st_ref)

    r = r_ref[...]
    k = k_ref[...]
    v = v_ref[...]
    seg = seg_ref[...]
    kk = k * kk_ref[...]
    kk = kk / jnp.maximum(jnp.sqrt(_dot_sel(kk * kk, seg)), 1e-12)
    w_log = -_softplus(-(w0_ref[...] + _bdot(jnp.tanh(wc_ref[...]), wup_ref[...]))) - 0.5
    lw = -jnp.exp(w_log)
    a = _sigmoid(a0_ref[...] + _bdot(ac_ref[...], aup_ref[...]))
    kd = k * (1.0 + (a - 1.0) * ka_ref[...])
    be = kk * a
    bonus_ref[...] = _dot_sel(r * kd * rk_ref[...], seg) * v

    cs = _sel_dot(incbd_ref[...], lw)
    tot = _sel_dot(totbd_ref[...], lw)
    e_neg = jnp.exp(-cs)
    e_end = jnp.exp(tot - cs)
    al_s = kk * jnp.exp(cs - lw)
    r_s = r * jnp.exp(cs)
    k_s = kd * e_neg
    b_s = be * e_neg
    ke_s = kd * e_end
    be_s = be * e_end

    L, c, nsub, npair, hd = LCH, RC, LCH // RC, B_H // 2, B_HD
    sm = sm_ref[...]
    im = im_ref[...]
    eye = eye_ref[...]
    lane = lax.broadcasted_iota(jnp.int32, (1, LANE), 1)
    hmask = [(lane < hd).astype(F32), (lane >= hd).astype(F32)]
    for i in range(nsub):
        gam_ref[i] = jnp.exp(tot[i * c:i * c + 1])

    heads = [(p, e) for p in range(npair) for e in range(2)]
    lanes = lambda a, p: a[:, p * LANE:(p + 1) * LANE]
    gg = [_dot_nt(jnp.concatenate([lanes(al_s, p), lanes(r_s, p)], axis=0) * hmask[e],
                  jnp.concatenate([lanes(k_s, p), lanes(b_s, p)], axis=0)) for p, e in heads]
    a_k = [g_[:L, :L] * sm for g_ in gg]
    n_m = [g_[:L, L:] * sm for g_ in gg]
    b_k = [g_[L:, :L] * im for g_ in gg]
    d_m = [g_[L:, L:] * im for g_ in gg]
    tinv = [eye - n_ for n_ in n_m]
    npow = n_m
    for _ in range(int(math.log2(c)) - 1):
        npow = [_bdot(n_, n_) for n_ in npow]
        tinv = [t_ + _bdot(t_, n_) for t_, n_ in zip(tinv, npow)]
    ve = [lanes(v, p) * hmask[e] for p, e in heads]
    at = [_bdot(t_, lanes(al_s, p) * hmask[e]) for t_, (p, e) in zip(tinv, heads)]
    av = [_bdot(a_, v_) for a_, v_ in zip(a_k, ve)]
    w1 = [_bdot(t_, x_) for t_, x_ in zip(tinv, av)]
    rt = [lanes(r_s, p) * hmask[e] - _bdot(d_, a_) for d_, a_, (p, e) in zip(d_m, at, heads)]
    bv = [_bdot(b_, v_) for b_, v_ in zip(b_k, ve)]
    y1 = [x_ - _bdot(d_, w_) for x_, d_, w_ in zip(bv, d_m, w1)]
    for p in range(npair):
        h0, h1 = 2 * p, 2 * p + 1
        vp, kep, bep = lanes(v, p), lanes(ke_s, p), lanes(be_s, p)
        for i in range(nsub):
            rw = slice(i * c, (i + 1) * c)
            lhs_ref[i, p] = jnp.concatenate([at[h0][rw], at[h1][rw], rt[h0][rw], rt[h1][rw]], axis=0).astype(BF16)
            bes = jnp.concatenate([bep[rw] * hmask[0], bep[rw] * hmask[1]], axis=0)
            bes_ref[i, p] = bes.astype(BF16)
            y1_ref[i, p] = jnp.concatenate([y1[h0][rw, :hd], y1[h1][rw, hd:]], axis=0)
            pt_ref[i, p] = _dot_tn(
                jnp.concatenate([vp[rw, :hd], vp[rw, hd:], w1[h0][rw, :hd], w1[h1][rw, hd:]], axis=0),
                jnp.concatenate([kep[rw] * hmask[0], kep[rw] * hmask[1], -bes], axis=0))

    for ii in range(nsub):
        i = jnp.where(d == 0, ii, nsub - 1 - ii)
        gam = gam_ref[i]
        stp = [st_ref[p] for p in range(npair)]
        x = [lax.dot_general(lhs_ref[i, p], stp[p].astype(BF16), (((1,), (1,)), ((), ())),
                             preferred_element_type=F32) for p in range(npair)]
        for p in range(npair):
            st_ref[p] = (stp[p] * gam[:, p * LANE:(p + 1) * LANE] + pt_ref[i, p]
                         - lax.dot_general(x[p][:2 * c].astype(BF16), bes_ref[i, p], (((0,), (0,)), ((), ())),
                                           preferred_element_type=F32))
        ys = []
        for p in range(npair):
            yp = x[p][2 * c:] + y1_ref[i, p]
            ys += [yp[:c], yp[c:]]
        y_ref[pl.ds(pl.multiple_of(i * c, c), c), :] = jnp.concatenate(ys, axis=-1)


def _rwkv_call(g, zc, zr, p, layer, rc):
    blk = g.chunk_block
    nsub, npair = LCH // RC, B_H // 2
    rkv = lambda c: pl.BlockSpec((LCH, B_W), lambda b, d, j: (blk(b, d, j), c))
    code = lambda c: pl.BlockSpec((LCH, LANE), lambda b, d, j: (blk(b, d, j), c))
    per_dir_vec = pl.BlockSpec((None, None, 1, B_W), lambda b, d, j: (layer, d, 0, 0))
    per_dir_mat = pl.BlockSpec((None, None, LANE, B_W), lambda b, d, j: (layer, d, 0, 0))
    vec = pl.BlockSpec((None, 1, B_W), lambda b, d, j: (layer, 0, 0))
    dirc = lambda a: pl.BlockSpec((None,) + a.shape[1:], lambda b, d, j: (d,) + (0,) * (a.ndim - 1))
    const = lambda a: pl.BlockSpec(a.shape, lambda b, d, j: (0,) * a.ndim)
    out = pl.BlockSpec((None, LCH, B_W), lambda b, d, j: (d, blk(b, d, j), 0))
    return pl.pallas_call(
        _rwkv_kernel,
        out_shape=(jax.ShapeDtypeStruct((2, g.rows, B_W), F32), jax.ShapeDtypeStruct((2, g.rows, B_W), F32)),
        grid=(g.batch, 2, g.chunk_steps),
        in_specs=[rkv(ZC_BR // B_W), rkv(ZC_BK // B_W), rkv(ZC_BV // B_W),
                  code(ZR_BW // LANE), code(ZR_BA // LANE),
                  per_dir_vec, per_dir_mat, per_dir_vec, per_dir_mat, vec, vec, vec,
                  dirc(rc["incbd"]), const(rc["totbd"]), dirc(rc["sm"]), dirc(rc["im"]), const(rc["seg"]),
                  const(rc["eye"])],
        out_specs=(out, out),
        scratch_shapes=[pltpu.VMEM((npair, B_HD, LANE), F32), pltpu.VMEM((nsub, npair, 4 * RC, LANE), BF16),
                        pltpu.VMEM((nsub, npair, 2 * RC, LANE), BF16), pltpu.VMEM((nsub, npair, B_HD, LANE), F32),
                        pltpu.VMEM((nsub, npair, 2 * RC, B_HD), F32), pltpu.VMEM((nsub, 1, B_W), F32)],
        compiler_params=_cparams(("arbitrary", "arbitrary", "arbitrary")), name="rwkv_scan",
    )(zc, zc, zc, zr, zr, p["w0"], p["w_up"], p["a0"], p["a_up"], p["kk"], p["ka"], p["rk"],
      rc["incbd"], rc["totbd"], rc["sm"], rc["im"], rc["seg"], rc["eye"])


def _post_kernel(ya_ref, yb_ref, bn_ref, yc_ref, ao_ref, bg_ref, co_ref, gna_ref, gnb_ref, gup_ref, gnc_ref,
                 sega_ref, segb_ref, segc_ref, o_ref):
    ha = ya_ref[0] + ya_ref[1]
    sega = sega_ref[...]
    hf = ha - _dot_sel(ha, sega) * (1.0 / A_HD)
    hn = hf * lax.rsqrt(_dot_sel(hf * hf, sega) * (1.0 / A_HD) + HEAD_EPS)
    h_a = _sigmoid(ao_ref[...]) * (hn * gna_ref[...])

    yb = yb_ref[0] + yb_ref[1]
    segb = segb_ref[...]
    yf = yb - _dot_sel(yb, segb) * (1.0 / B_HD)
    yn = yf * lax.rsqrt(_dot_sel(yf * yf, segb) * (1.0 / B_HD) + B_GN_EPS)
    ob = yn * gnb_ref[0] + gnb_ref[1] + bn_ref[0] + bn_ref[1]
    h_b = ob * _bdot(_sigmoid(bg_ref[...]), gup_ref[...])

    oc = yc_ref[0] + yc_ref[1]
    on = oc * lax.rsqrt(_dot_sel(oc * oc, segc_ref[...]) * (1.0 / C_DV) + HEAD_EPS) * gnc_ref[...]
    co = co_ref[...]
    h_c = on * (co * _sigmoid(co))
    o_ref[...] = jnp.concatenate([h_a, h_b, h_c], axis=-1).astype(BF16)


def _post_call(g, ya, yb, bonus, yc, zr, p, layer, segs):
    tm = TM_S
    two = lambda w: pl.BlockSpec((2, tm, w), lambda i: (0, i, 0))
    zcol = lambda w, off: pl.BlockSpec((tm, w), lambda i: (i, off // w))
    const = lambda a: pl.BlockSpec(a.shape, lambda i: (0,) * a.ndim)
    return pl.pallas_call(
        _post_kernel, out_shape=jax.ShapeDtypeStruct((g.rows, H_W), BF16), grid=(g.rows // tm,),
        in_specs=[two(A_W), two(B_W), two(B_W), two(C_VW),
                  zcol(A_W, ZR_AO), zcol(LANE, ZR_BG), zcol(C_VW, ZR_CO),
                  pl.BlockSpec((None, 1, A_W), lambda i: (layer, 0, 0)),
                  pl.BlockSpec((None, 2, 1, B_W), lambda i: (layer, 0, 0, 0)),
                  pl.BlockSpec((None, LANE, B_W), lambda i: (layer, 0, 0)),
                  pl.BlockSpec((None, 1, C_VW), lambda i: (layer, 0, 0)),
                  const(segs[0]), const(segs[1]), const(segs[2])],
        out_specs=pl.BlockSpec((tm, H_W), lambda i: (i, 0)),
        compiler_params=_cparams(("parallel",)), name="mixer_post",
    )(ya, yb, bonus, yc, zr, zr, zr, p["gn_a"], p["gn_b"], p["g_up"], p["gn_c"], *segs)


def _merge_kernel(h_ref, mg_ref, x_ref, gt_ref, lng_ref, lnb_ref, wcat_ref, wout_ref, o_ref, *, alpha):
    h = h_ref[...]
    mg = mg_ref[...]
    o1, o2 = A_W, A_W + B_W
    merged = (_sigmoid(mg[:, :D]) * jnp.dot(h[:, :o1], wcat_ref[:o1], preferred_element_type=F32)
              + _sigmoid(mg[:, D:2 * D]) * jnp.dot(h[:, o1:o2], wcat_ref[o1:o2], preferred_element_type=F32)
              + _sigmoid(mg[:, 2 * D:]) * jnp.dot(h[:, o2:], wcat_ref[o2:], preferred_element_type=F32))
    y = jnp.dot(merged.astype(BF16), wout_ref[...], preferred_element_type=F32)
    o_ref[...] = _ln(alpha * x_ref[...] + gt_ref[...] * y) * lng_ref[...] + lnb_ref[...]


def _ln_spec(layer, sub, which):
    return pl.BlockSpec((None, None, None, 1, D), lambda i, *_: (layer, sub, which, 0, 0))


def _merge_call(g, h, zr, x, modflat, ln_post, wcat, wout, layer, alpha):
    tm = TM_S
    return pl.pallas_call(
        functools.partial(_merge_kernel, alpha=alpha), out_shape=jax.ShapeDtypeStruct((g.rows, D), F32),
        grid=(g.rows // tm,),
        in_specs=[pl.BlockSpec((tm, H_W), lambda i: (i, 0)),
                  pl.BlockSpec((tm, 3 * D), lambda i: (i, 0)),
                  pl.BlockSpec((tm, D), lambda i: (i, 0)),
                  g.mod_spec(layer, 2, tm), _ln_spec(layer, 0, 0), _ln_spec(layer, 0, 1),
                  pl.BlockSpec((None, H_W, D), lambda i: (layer, 0, 0), pipeline_mode=pl.Buffered(1)),
                  pl.BlockSpec((None, D, D), lambda i: (layer, 0, 0), pipeline_mode=pl.Buffered(1))],
        out_specs=pl.BlockSpec((tm, D), lambda i: (i, 0)),
        compiler_params=_cparams(("parallel",)), name="branch_merge",
    )(h, zr, x, modflat, ln_post, ln_post, wcat, wout)


def _ffn_kernel(x_ref, sh_ref, sc_ref, gt_ref, lng_ref, lnb_ref, wg_ref, wu_ref, wd_ref, o_ref, u_ref, acc_ref,
                *, alpha):
    j = pl.program_id(1)

    @pl.when(j == 0)
    def _():
        u_ref[...] = (_ln(x_ref[...]) * (1.0 + sc_ref[...]) + sh_ref[...]).astype(BF16)
        acc_ref[...] = jnp.zeros_like(acc_ref)

    u = u_ref[...]
    gate = jnp.dot(u, wg_ref[...], preferred_element_type=F32)
    up = jnp.dot(u, wu_ref[...], preferred_element_type=F32)
    hid = (gate * _sigmoid(gate)) * up
    acc_ref[...] += jnp.dot(hid.astype(BF16), wd_ref[...], preferred_element_type=F32)

    @pl.when(j == pl.num_programs(1) - 1)
    def _():
        o_ref[...] = _ln(alpha * x_ref[...] + gt_ref[...] * acc_ref[...]) * lng_ref[...] + lnb_ref[...]


def _ffn_call(g, x, modflat, ln_post, wg, wu, wd, layer, idx, alpha):
    tf = 512
    return pl.pallas_call(
        functools.partial(_ffn_kernel, alpha=alpha), out_shape=jax.ShapeDtypeStruct((g.rows, D), F32),
        grid=(g.rows // TM, D_FFP // tf),
        in_specs=[pl.BlockSpec((TM, D), lambda i, j: (i, 0)),
                  g.mod_spec(layer, 3, TM), g.mod_spec(layer, 4, TM), g.mod_spec(layer, 5, TM),
                  _ln_spec(layer, 1, 0), _ln_spec(layer, 1, 1),
                  pl.BlockSpec((None, D, tf), lambda i, j: (idx, 0, j)),
                  pl.BlockSpec((None, D, tf), lambda i, j: (idx, 0, j)),
                  pl.BlockSpec((None, tf, D), lambda i, j: (idx, j, 0))],
        out_specs=pl.BlockSpec((TM, D), lambda i, j: (i, 0)),
        scratch_shapes=[pltpu.VMEM((TM, D), BF16), pltpu.VMEM((TM, D), F32)],
        compiler_params=_cparams(("parallel", "arbitrary")), name="ffn_swiglu",
    )(x, modflat, modflat, modflat, ln_post, ln_post, wg, wu, wd)


def _router_kernel(x_ref, sh_ref, sc_ref, w_ref, b_ref, u_ref, sel_ref):
    u = _ln(x_ref[...]) * (1.0 + sc_ref[...]) + sh_ref[...]
    u_ref[...] = u
    lane = lax.broadcasted_iota(jnp.int32, (1, LANE), 1)
    logits = jnp.where(lane < N_EXPERTS, _dot3(u, w_ref[...]) + b_ref[...], -jnp.inf)
    m1 = jnp.max(logits, axis=-1, keepdims=True)
    i1 = jnp.min(jnp.where(logits == m1, lane, LANE), axis=-1, keepdims=True)
    rest = jnp.where(lane == i1, -jnp.inf, logits)
    m2 = jnp.max(rest, axis=-1, keepdims=True)
    i2 = jnp.min(jnp.where(rest == m2, lane, LANE), axis=-1, keepdims=True)
    e2 = jnp.exp(m2 - m1)
    p1 = 1.0 / (1.0 + e2)
    sel_ref[...] = (jnp.where(lane == 0, i1.astype(F32), 0.0) + jnp.where(lane == 1, i2.astype(F32), 0.0)
                    + jnp.where(lane == 2, p1, 0.0) + jnp.where(lane == 3, e2 * p1, 0.0))


def _router_call(g, x, modflat, router, router_b, layer, idx):
    return pl.pallas_call(
        _router_kernel,
        out_shape=(jax.ShapeDtypeStruct((g.rows, D), F32), jax.ShapeDtypeStruct((g.rows, LANE), F32)),
        grid=(g.rows // TM,),
        in_specs=[pl.BlockSpec((TM, D), lambda i: (i, 0)), g.mod_spec(layer, 3, TM), g.mod_spec(layer, 4, TM),
                  pl.BlockSpec((None, D, LANE), lambda i: (idx, 0, 0)),
                  pl.BlockSpec((None, 1, LANE), lambda i: (idx, 0, 0))],
        out_specs=(pl.BlockSpec((TM, D), lambda i: (i, 0)), pl.BlockSpec((TM, LANE), lambda i: (i, 0))),
        compiler_params=_cparams(("parallel",)), name="moe_router",
    )(x, modflat, modflat, router, router_b)


def _route(sel, n_slots):
    t = sel.shape[0]
    expert = jnp.concatenate([sel[:, 0], sel[:, 1]]).astype(jnp.int32)
    prob = jnp.concatenate([sel[:, 2], sel[:, 3]])
    onehot = (expert[:, None] == jnp.arange(N_EXPERTS, dtype=jnp.int32)[None, :]).astype(jnp.int32)
    csum = jnp.cumsum(onehot, axis=0)
    rank = jnp.sum(csum * onehot, axis=1) - 1
    count = csum[-1]
    padded = ((count + TM - 1) // TM) * TM
    group_end = jnp.cumsum(padded)
    pos = jnp.sum(onehot * (group_end - padded)[None, :], axis=1) + rank
    token = jnp.concatenate([jnp.arange(t, dtype=jnp.int32)] * 2)
    src = jnp.zeros((n_slots,), jnp.int32).at[pos].set(token)
    prob_sorted = jnp.zeros((n_slots,), F32).at[pos].set(prob)
    tile_start = jnp.arange(n_slots // TM, dtype=jnp.int32) * TM
    tile_expert = jnp.sum((tile_start[:, None] >= group_end[None, :]).astype(jnp.int32), axis=1)
    tile_valid = (tile_expert < N_EXPERTS).astype(jnp.int32)
    return src, prob_sorted, jnp.minimum(tile_expert, N_EXPERTS - 1), tile_valid, pos


def _row_gather_kernel(idx_ref, src_ref, o_ref, sem, *, n):
    def row_copy(r):
        return pltpu.make_async_copy(src_ref.at[pl.ds(idx_ref[0, 0, r], 1)], o_ref.at[pl.ds(r, 1)], sem)

    def start(r, carry):
        row_copy(r).start()
        return carry

    def wait(r, carry):
        row_copy(r).wait()
        return carry

    lax.fori_loop(0, n, start, 0)
    lax.fori_loop(0, n, wait, 0)


def _row_gather_call(src, idx):
    n = idx.shape[0]
    return pl.pallas_call(
        functools.partial(_row_gather_kernel, n=TM), out_shape=jax.ShapeDtypeStruct((n, D), F32),
        grid=(n // TM,),
        in_specs=[pl.BlockSpec((1, 1, TM), lambda i: (i, 0, 0), memory_space=pltpu.SMEM),
                  pl.BlockSpec(memory_space=pl.ANY)],
        out_specs=pl.BlockSpec((TM, D), lambda i: (i, 0)),
        scratch_shapes=[pltpu.SemaphoreType.DMA(())],
        compiler_params=_cparams(("arbitrary",)), name="moe_gather",
    )(idx.reshape(n // TM, 1, TM), src)


def _moe_group_kernel(te_ref, valid_ref, xs_ref, pr_ref, wg_ref, wu_ref, wd_ref, o_ref, u_ref, acc_ref):
    i = pl.program_id(0)
    j = pl.program_id(1)

    @pl.when(j == 0)
    def _():
        u_ref[...] = xs_ref[...].astype(BF16)
        acc_ref[...] = jnp.zeros_like(acc_ref)

    @pl.when(valid_ref[i] > 0)
    def _():
        u = u_ref[...]
        gate = jnp.dot(u, wg_ref[...], preferred_element_type=F32)
        up = jnp.dot(u, wu_ref[...], preferred_element_type=F32)
        hid = (gate * _sigmoid(gate)) * up
        acc_ref[...] += jnp.dot(hid.astype(BF16), wd_ref[...], preferred_element_type=F32)

    @pl.when(j == pl.num_programs(1) - 1)
    def _():
        o_ref[...] = pr_ref[...] * acc_ref[...]


def _moe_group_call(xs, prob_sorted, tile_expert, tile_valid, wg, wu, wd, idx):
    tf = 512
    n = xs.shape[0]
    fcol = lambda i, j, te, va: jnp.where(va[i] > 0, j, 0)
    grid_spec = pltpu.PrefetchScalarGridSpec(
        num_scalar_prefetch=2, grid=(n // TM, D_FFP // tf),
        in_specs=[pl.BlockSpec((TM, D), lambda i, j, te, va: (i, 0)),
                  pl.BlockSpec((TM, 1), lambda i, j, te, va: (i, 0)),
                  pl.BlockSpec((None, None, D, tf), lambda i, j, te, va: (idx, te[i], 0, fcol(i, j, te, va))),
                  pl.BlockSpec((None, None, D, tf), lambda i, j, te, va: (idx, te[i], 0, fcol(i, j, te, va))),
                  pl.BlockSpec((None, None, tf, D), lambda i, j, te, va: (idx, te[i], fcol(i, j, te, va), 0))],
        out_specs=pl.BlockSpec((TM, D), lambda i, j, te, va: (i, 0)),
        scratch_shapes=[pltpu.VMEM((TM, D), BF16), pltpu.VMEM((TM, D), F32)])
    return pl.pallas_call(
        _moe_group_kernel, out_shape=jax.ShapeDtypeStruct((n, D), F32), grid_spec=grid_spec,
        compiler_params=_cparams(("arbitrary", "arbitrary")), name="moe_experts",
    )(tile_expert, tile_valid, xs, prob_sorted.reshape(n, 1), wg, wu, wd)


def _moe_combine_kernel(pos_ref, ys_ref, x_ref, gt_ref, lng_ref, lnb_ref, o_ref, buf_ref, sem, *, alpha):
    def row_copy(r):
        return pltpu.make_async_copy(ys_ref.at[pl.ds(pos_ref[0, 0, r], 1)], buf_ref.at[pl.ds(r, 1)], sem)

    def start(r, carry):
        row_copy(r).start()
        return carry

    def wait(r, carry):
        row_copy(r).wait()
        return carry

    lax.fori_loop(0, 2 * TM, start, 0)
    lax.fori_loop(0, 2 * TM, wait, 0)
    f = buf_ref[:TM, :] + buf_ref[TM:, :]
    o_ref[...] = _ln(alpha * x_ref[...] + gt_ref[...] * f) * lng_ref[...] + lnb_ref[...]


def _moe_combine_call(g, ys, pos, x, modflat, ln_post, layer, alpha):
    nt = g.rows // TM
    pos2 = jnp.concatenate([pos[:g.rows].reshape(nt, 1, TM), pos[g.rows:].reshape(nt, 1, TM)], axis=-1)
    return pl.pallas_call(
        functools.partial(_moe_combine_kernel, alpha=alpha), out_shape=jax.ShapeDtypeStruct((g.rows, D), F32),
        grid=(nt,),
        in_specs=[pl.BlockSpec((1, 1, 2 * TM), lambda i: (i, 0, 0), memory_space=pltpu.SMEM),
                  pl.BlockSpec(memory_space=pl.ANY),
                  pl.BlockSpec((TM, D), lambda i: (i, 0)),
                  g.mod_spec(layer, 5, TM), _ln_spec(layer, 1, 0), _ln_spec(layer, 1, 1)],
        out_specs=pl.BlockSpec((TM, D), lambda i: (i, 0)),
        scratch_shapes=[pltpu.VMEM((2 * TM, D), F32), pltpu.SemaphoreType.DMA(())],
        compiler_params=_cparams(("arbitrary",)), name="moe_combine",
    )(pos2, ys, x, modflat, ln_post, ln_post)


def kernel(x, c, ctx, c_ctx, w_mod, b_mod, w_in, conv_k, mlstm_gate_b, mlstm_gn, rwkv_w0, rwkv_w_up, rwkv_a0,
           rwkv_a_up, rwkv_g_up, rwkv_kk, rwkv_ka, rwkv_rk, rwkv_gn, gla_gate_up, gla_gate_b, gla_gn, w_branch_a,
           w_branch_b, w_branch_c, w_out, ln_post, ffn_w_gate, ffn_w_up, ffn_w_down, moe_router, moe_router_b,
           moe_w_gate, moe_w_up, moe_w_down):
    batch, seq, _ = x.shape
    ctx_len = ctx.shape[1]
    depth = w_mod.shape[0]
    alpha = (2.0 * depth) ** 0.25
    g = _Geom(batch, seq, ctx_len)

    n_conv = 5504
    w_conv = _pack_conv_cols(w_in[..., :n_conv]).astype(BF16)
    w_rest = _pack_rest_cols(w_in[..., n_conv:]).astype(BF16)
    conv_w = _pack_conv_cols(conv_k.reshape(depth, 9, n_conv))
    gate_b = _pad_last(mlstm_gate_b.reshape(depth, 1, 16), LANE)

    def rows_at(a, per, n):
        out = jnp.zeros((depth, 2, n, a.shape[-1]), a.dtype)
        for d in range(2):
            out = out.at[:, d, d * per:(d + 1) * per].set(a[:, d])
        return out

    rw = dict(w0=rwkv_w0.reshape(depth, 2, 1, B_W), w_up=rows_at(rwkv_w_up, B_W_LORA, LANE).astype(BF16),
              a0=rwkv_a0.reshape(depth, 2, 1, B_W), a_up=rows_at(rwkv_a_up, B_A_LORA, LANE).astype(BF16),
              kk=rwkv_kk.reshape(depth, 1, B_W), ka=rwkv_ka.reshape(depth, 1, B_W),
              rk=rwkv_rk.reshape(depth, 1, B_W))
    gla_up = _pad_last(rows_at(gla_gate_up, C_G_LORA, LANE), C_KP).astype(BF16)
    gla_b = _pad_last(gla_gate_b.reshape(depth, 2, 1, C_KW), C_KP)
    post_p = dict(gn_a=_pad_heads(mlstm_gn, A_H, A_HD, A_HP).reshape(depth, 1, A_W),
                  gn_b=rwkv_gn.reshape(depth, 2, 1, B_W), g_up=rwkv_g_up.astype(BF16),
                  gn_c=gla_gn.reshape(depth, 1, C_VW))
    wa = _pad_heads(jnp.swapaxes(w_branch_a, 1, 2), A_H, A_HD, A_HP)
    wcat = jnp.concatenate([jnp.swapaxes(wa, 1, 2), w_branch_b, w_branch_c], axis=1).astype(BF16)
    wout = w_out.astype(BF16)
    lnp = ln_post.reshape(depth, 2, 2, 1, D)
    padf = lambda a: _pad_last(a, D_FFP).astype(BF16)
    padr = lambda a: jnp.pad(a, [(0, 0)] * (a.ndim - 2) + [(0, D_FFP - D_FF), (0, 0)]).astype(BF16)
    f_wg, f_wu, f_wd = padf(ffn_w_gate), padf(ffn_w_up), padr(ffn_w_down)
    m_wg, m_wu, m_wd = padf(moe_w_gate), padf(moe_w_up), padr(moe_w_down)
    router = _pad_last(moe_router, LANE)
    router_b = _pad_last(moe_router_b.reshape(-1, 1, N_EXPERTS), LANE)

    inc_a = _mlstm_consts()
    gla_c = _gla_consts()
    rwkv_c = _rwkv_consts()
    segs = _seg_consts()

    cvec = jnp.concatenate([c, c_ctx[None, :], jnp.zeros((8 - batch - 1, D), F32)], axis=0)
    modflat = _mod_call(cvec, w_mod, b_mod).reshape(depth * 8, 1, 6 * D)

    xs = jnp.concatenate([x.reshape(batch * seq, D), ctx.reshape(batch * ctx_len, D)], axis=0)
    for l in range(depth):
        zc = _conv_call(g, _inproj_call(g, xs, modflat, l, w_conv, 2176, "in_proj_conv"), conv_w, l)
        zr = _inproj_call(g, xs, modflat, l, w_rest, 1408, "in_proj_rest")
        ya = _mlstm_call(g, zc, zr, gate_b, l, inc_a)
        yb, bonus = _rwkv_call(g, zc, zr, rw, l, rwkv_c)
        yc = _gla_call(g, zc, zr, gla_up, gla_b, l, gla_c)
        h = _post_call(g, ya, yb, bonus, yc, zr, post_p, l, segs)
        xs = _merge_call(g, h, zr, xs, modflat, lnp, wcat, wout, l, alpha)
        i = l // 2
        if l % 2 == 0:
            xs = _ffn_call(g, xs, modflat, lnp, f_wg, f_wu, f_wd, l, i, alpha)
        else:
            u, sel = _router_call(g, xs, modflat, router, router_b, l, i)
            n_slots = -(-(2 * g.rows + N_EXPERTS * (TM - 1)) // TM) * TM
            src, prob_sorted, tile_expert, tile_valid, pos = _route(sel, n_slots)
            ys = _moe_group_call(_row_gather_call(u, src), prob_sorted, tile_expert, tile_valid, m_wg, m_wu, m_wd, i)
            xs = _moe_combine_call(g, ys, pos, xs, modflat, lnp, l, alpha)
    return xs[:batch * seq].reshape(batch, seq, D)
```

```python
import functools
import math

import jax
import jax.numpy as jnp
import numpy as np
from jax import lax
from jax.experimental import pallas as pl
from jax.experimental.pallas import tpu as pltpu

F32 = jnp.float32
BF16 = jnp.bfloat16

D = 2048
GRID_W = 64
A_H, A_HD, A_HP = 4, 192, 256
A_W = A_H * A_HP
B_H, B_HD = 10, 64
B_W = B_H * B_HD
B_W_LORA, B_A_LORA, B_G_LORA = 64, 32, 128
B_GN_EPS = 64e-5
C_H, C_DK, C_DV = 4, 80, 160
C_KW, C_KP, C_VW = C_H * C_DK, 384, C_H * C_DV
C_G_LORA = 16
C_TAU = 16.0
D_FF = 5504
TF = 512
N_FF_TILES = -(-D_FF // TF)
N_EXPERTS = 8
LN_EPS = 1e-6
HEAD_EPS = 1e-5
LANE = 128

ZC_AQ, ZC_AK, ZC_AV = 0, 1024, 2048
ZC_BR, ZC_BK, ZC_BV, ZC_CV = 3200, 3840, 4480, 5120
ZC_CQ, ZC_CK = 5760, 6144
ZC_W = 6656
ZR_MG, ZR_AO, ZR_AG, ZR_BW, ZR_BA, ZR_BG, ZR_CO, ZR_CG = 0, 6144, 7168, 7296, 7424, 7552, 7680, 8320
ZR_W = 8448
H_W = A_W + B_W + C_VW

TM = 512
TM_S = 256
LCH = 128
RC = 64
VMEM_LIMIT = 56 * 1024 * 1024


def _cparams(sem):
    return pltpu.CompilerParams(dimension_semantics=sem, vmem_limit_bytes=VMEM_LIMIT)


def _ln(a):
    mu = jnp.mean(a, axis=-1, keepdims=True)
    d = a - mu
    var = jnp.mean(d * d, axis=-1, keepdims=True)
    return d * lax.rsqrt(var + LN_EPS)


def _sigmoid(x):
    return 1.0 / (1.0 + jnp.exp(-x))


def _softplus(x):
    return jnp.maximum(x, 0.0) + jnp.log(1.0 + jnp.exp(-jnp.abs(x)))


def _bdot(a, b):
    return jnp.dot(a.astype(BF16), b.astype(BF16), preferred_element_type=F32)


def _dot_nt(a, b):
    return lax.dot_general(a.astype(BF16), b.astype(BF16), (((1,), (1,)), ((), ())), preferred_element_type=F32)


def _dot_tn(a, b):
    return lax.dot_general(a.astype(BF16), b.astype(BF16), (((0,), (0,)), ((), ())), preferred_element_type=F32)


def _split2(x):
    x1 = x.astype(BF16)
    return x1, (x - x1.astype(F32)).astype(BF16)


def _split3(x):
    x1 = x.astype(BF16)
    r = x - x1.astype(F32)
    x2 = r.astype(BF16)
    return x1, x2, (r - x2.astype(F32)).astype(BF16)


def _sel_dot(m, x):
    return sum(jnp.dot(m, p, preferred_element_type=F32) for p in _split3(x))


def _dot_sel(x, m):
    return sum(jnp.dot(p, m, preferred_element_type=F32) for p in _split2(x))


def _dot3(a, b):
    a1, a2 = _split2(a)
    b1, b2 = _split2(b)
    return (jnp.dot(a1, b1, preferred_element_type=F32) + jnp.dot(a1, b2, preferred_element_type=F32)
            + jnp.dot(a2, b1, preferred_element_type=F32))


def _dir_pos(n):
    return np.stack([np.arange(n), n - 1 - np.arange(n)])


def _mlstm_consts():
    pos = _dir_pos(LCH)
    inc = (pos[:, None, :] <= pos[:, :, None]).astype(np.float32)
    return jnp.asarray(inc, BF16)


def _gla_consts():
    L = LCH
    pos = _dir_pos(L)
    widths = [L >> i for i in range(int(math.log2(L)))]
    nl = len(widths)
    wmat = np.zeros((2, (nl + 2) * L + 8, L), np.float32)
    masks = np.zeros((2, (nl + 1) * C_H * L, L), np.float32)
    for d in range(2):
        p = pos[d]
        for li, w in enumerate(widths):
            half = w // 2
            blk, off = p // w, p % w
            mid = blk * w + half
            isq, isk = off >= half, off < half
            eq = (p[None, :] >= mid[:, None]) & (p[None, :] <= p[:, None]) & isq[:, None]
            ek = (p[None, :] > p[:, None]) & (p[None, :] <= mid[:, None] - 1) & isk[:, None]
            wmat[d, li * L:(li + 1) * L] = eq | ek
            m = (blk[:, None] == blk[None, :]) & isq[:, None] & isk[None, :]
            masks[d, li * C_H * L:(li + 1) * C_H * L] = np.tile(m, (C_H, 1))
        wmat[d, nl * L:(nl + 1) * L] = p[None, :] <= p[:, None]
        wmat[d, (nl + 1) * L:(nl + 2) * L] = p[None, :] > p[:, None]
        wmat[d, (nl + 2) * L:] = 1.0
        masks[d, nl * C_H * L:] = np.tile(np.eye(L), (C_H, 1))
    qmask = np.zeros((C_H, 1, C_KP), np.float32)
    vmask = np.zeros((C_H, 1, C_VW), np.float32)
    bd = np.zeros((C_VW, C_KP), np.float32)
    for h in range(C_H):
        qmask[h, 0, h * C_DK:(h + 1) * C_DK] = 1.0
        vmask[h, 0, h * C_DV:(h + 1) * C_DV] = 1.0
        bd[h * C_DV:(h + 1) * C_DV, h * C_DK:(h + 1) * C_DK] = 1.0
    return (jnp.asarray(wmat, BF16), jnp.asarray(masks, F32), jnp.asarray(qmask, F32), jnp.asarray(vmask, F32),
            jnp.asarray(bd, F32), nl)


def _rwkv_consts():
    c, H, L = RC, B_H, LCH
    pos = _dir_pos(c)
    inc = (pos[:, None, :] <= pos[:, :, None]).astype(np.float32)
    nsub = L // c
    incbd = np.zeros((2, L, L), np.float32)
    for d in range(2):
        for i in range(nsub):
            incbd[d, i * c:(i + 1) * c, i * c:(i + 1) * c] = inc[d]
    sub = np.arange(L) // c
    same = sub[:, None] == sub[None, :]
    totbd = same.astype(np.float32)
    pp = np.stack([np.tile(pos[d], nsub) for d in range(2)])
    sm = (same[None] & (pp[:, None, :] < pp[:, :, None])).astype(np.float32)
    im = (same[None] & (pp[:, None, :] <= pp[:, :, None])).astype(np.float32)
    lane_head = np.repeat(np.arange(H), B_HD)
    seg = (lane_head[:, None] == lane_head[None, :]).astype(np.float32)
    return dict(incbd=jnp.asarray(incbd, BF16), totbd=jnp.asarray(totbd, BF16), sm=jnp.asarray(sm, F32),
                im=jnp.asarray(im, F32), seg=jnp.asarray(seg, BF16), eye=jnp.eye(L, dtype=F32))


def _seg_consts():
    va = np.arange(A_HP) < A_HD
    sega = va[:, None] & va[None, :]
    lb = np.arange(LANE) // B_HD
    segb = lb[:, None] == lb[None, :]
    lc = np.arange(C_VW) // C_DV
    segc = lc[:, None] == lc[None, :]
    return jnp.asarray(sega, BF16), jnp.asarray(segb, BF16), jnp.asarray(segc, BF16)


def _pad_last(a, n):
    return jnp.pad(a, [(0, 0)] * (a.ndim - 1) + [(0, n - a.shape[-1])])


def _pad_heads(a, nh, hd, hp):
    a = a.reshape(a.shape[:-1] + (nh, hd))
    a = jnp.pad(a, [(0, 0)] * (a.ndim - 1) + [(0, hp - hd)])
    return a.reshape(a.shape[:-2] + (nh * hp,))


def _pack_conv_cols(w):
    s = lambda a, b: w[..., a:b]
    z = lambda n: jnp.zeros(w.shape[:-1] + (n,), w.dtype)
    parts = [_pad_heads(s(0, 768), A_H, A_HD, A_HP), _pad_heads(s(768, 1536), A_H, A_HD, A_HP),
             _pad_heads(s(1536, 2304), A_H, A_HD, A_HP), z(128),
             s(2304, 2944), s(2944, 3584), s(3584, 4224), s(4864, 5504),
             _pad_last(s(4224, 4544), C_KP), _pad_last(s(4544, 4864), C_KP), z(128)]
    return jnp.concatenate(parts, axis=-1)


def _pack_rest_cols(w):
    s = lambda a, b: w[..., a:b]
    parts = [s(0, 6144), _pad_heads(s(6144, 6912), A_H, A_HD, A_HP), _pad_last(s(6912, 6928), LANE),
             s(6928, 7056), _pad_last(s(7056, 7120), LANE), s(7120, 7248), s(7280, 7920),
             _pad_last(s(7248, 7280), LANE)]
    return jnp.concatenate(parts, axis=-1)


def _mod_kernel(c_ref, w_ref, b_ref, o_ref):
    c = c_ref[...]
    s = c * _sigmoid(c)
    o_ref[...] = jnp.dot(s, w_ref[...], preferred_element_type=F32) + b_ref[...]


def _mod_call(cvec, w_mod, b_mod):
    depth, _, n6 = w_mod.shape
    tn = 1536
    return pl.pallas_call(
        _mod_kernel, out_shape=jax.ShapeDtypeStruct((depth, 8, n6), F32), grid=(depth, n6 // tn),
        in_specs=[pl.BlockSpec((8, D), lambda l, j: (0, 0)),
                  pl.BlockSpec((None, D, tn), lambda l, j: (l, 0, j)),
                  pl.BlockSpec((None, 1, tn), lambda l, j: (l, 0, j))],
        out_specs=pl.BlockSpec((None, 8, tn), lambda l, j: (l, 0, j)),
        compiler_params=_cparams(("parallel", "parallel")), name="mod_vectors",
    )(cvec, w_mod, b_mod.reshape(depth, 1, n6))


class _Geom:
    def __init__(self, batch, seq, ctx_len):
        assert seq % TM == 0 and (batch * ctx_len) % TM == 0 and TM % ctx_len == 0 and TM % GRID_W == 0
        assert ctx_len % LCH == 0 and seq % LCH == 0 and ctx_len & (ctx_len - 1) == 0
        self.batch, self.seq, self.ctx_len = batch, seq, ctx_len
        self.rows = batch * (seq + ctx_len)
        self.n_lat_rows = batch * seq

    def mod_row(self, i, tm):
        return jnp.minimum(i // (self.seq // tm), self.batch)

    def mod_spec(self, layer, k, tm):
        return pl.BlockSpec((None, 1, D), lambda i, *_: (layer * 8 + self.mod_row(i, tm), 0, k))

    def chunk_block(self, b, d, j):
        ncc, nlc = self.ctx_len // LCH, self.seq // LCH
        is_ctx = j < ncc
        jj = jnp.where(is_ctx, j, j - ncc)
        n = jnp.where(is_ctx, ncc, nlc)
        loc = jnp.where(d == 0, jj, n - 1 - jj)
        return jnp.where(is_ctx, self.n_lat_rows // LCH + b * ncc + loc, b * nlc + loc)

    @property
    def chunk_steps(self):
        return (self.ctx_len + self.seq) // LCH


def _inproj_kernel(x_ref, sh_ref, sc_ref, w_ref, o_ref, u_ref):
    @pl.when(pl.program_id(1) == 0)
    def _():
        u_ref[...] = (_ln(x_ref[...]) * (1.0 + sc_ref[...]) + sh_ref[...]).astype(BF16)

    o_ref[...] = jnp.dot(u_ref[...], w_ref[...], preferred_element_type=F32)


def _inproj_call(g, x, modflat, layer, w, tn, name):
    n = w.shape[-1]
    return pl.pallas_call(
        _inproj_kernel, out_shape=jax.ShapeDtypeStruct((g.rows, n), F32), grid=(g.rows // TM, n // tn),
        in_specs=[pl.BlockSpec((TM, D), lambda i, j: (i, 0)),
                  g.mod_spec(layer, 0, TM), g.mod_spec(layer, 1, TM),
                  pl.BlockSpec((None, D, tn), lambda i, j: (layer, 0, j))],
        out_specs=pl.BlockSpec((TM, tn), lambda i, j: (i, j)),
        scratch_shapes=[pltpu.VMEM((TM, D), BF16)],
        compiler_params=_cparams(("parallel", "arbitrary")), name=name,
    )(x, modflat, modflat, w)


def _inproj_conv_kernel(prev_ref, main_ref, next_ref, sh_ref, sc_ref, w_ref, cw_ref, o_ref, u_ref, *, n_lat_tiles,
                        tiles_per_batch, ctx_len, tc):
    i = pl.program_id(0)

    @pl.when(pl.program_id(1) == 0)
    def _():
        mod = lambda a: (_ln(a) * (1.0 + sc_ref[...]) + sh_ref[...]).astype(BF16)
        u_ref[:GRID_W, :] = mod(prev_ref[...])
        u_ref[GRID_W:GRID_W + TM, :] = mod(main_ref[...])
        u_ref[GRID_W + TM:, :] = mod(next_ref[...])

    is_ctx = i >= n_lat_tiles
    lat = jnp.logical_not(is_ctx)
    ib = i % tiles_per_batch
    top_ok = jnp.logical_and(ib != 0, lat)
    bot_ok = jnp.logical_and(ib != tiles_per_batch - 1, lat)
    t = lax.broadcasted_iota(jnp.int32, (TM, 1), 0)
    up_ok = jnp.logical_and(lat, jnp.logical_or(t >= GRID_W, top_ok))
    dn_ok = jnp.logical_and(lat, jnp.logical_or(t < TM - GRID_W, bot_ok))
    col = jnp.where(is_ctx, t & (ctx_len - 1), t & (GRID_W - 1))
    left_ok = col != 0
    right_ok = col != jnp.where(is_ctx, ctx_len - 1, GRID_W - 1)
    u = u_ref[...]
    step = 2 * LANE
    z_all = jnp.dot(u, w_ref[...], preferred_element_type=F32)
    for c0 in range(0, tc, step):
        cs = slice(c0, min(c0 + step, tc))
        z = z_all[:, cs]
        w = cw_ref[:, cs]
        up = jnp.where(up_ok, z[:TM], 0.0)
        main = z[GRID_W:GRID_W + TM]
        dn = jnp.where(dn_ok, z[2 * GRID_W:], 0.0)
        a0 = w[0:1] * up + w[3:4] * main + w[6:7] * dn
        a1 = w[1:2] * up + w[4:5] * main + w[7:8] * dn
        a2 = w[2:3] * up + w[5:6] * main + w[8:9] * dn
        o_ref[:, cs] = (a1 + jnp.where(left_ok, pltpu.roll(a0, 1, 0), 0.0)
                        + jnp.where(right_ok, pltpu.roll(a2, TM - 1, 0), 0.0))


def _inproj_conv_call(g, x, modflat, layer, w, conv_w):
    tc = 1664
    hb = TM // GRID_W
    nb64 = g.rows // GRID_W
    kern = functools.partial(_inproj_conv_kernel, n_lat_tiles=g.n_lat_rows // TM, tiles_per_batch=g.seq // TM,
                             ctx_len=g.ctx_len, tc=tc)
    return pl.pallas_call(
        kern, out_shape=jax.ShapeDtypeStruct((g.rows, ZC_W), F32), grid=(g.rows // TM, ZC_W // tc),
        in_specs=[pl.BlockSpec((GRID_W, D), lambda i, j: (jnp.maximum(i * hb - 1, 0), 0)),
                  pl.BlockSpec((TM, D), lambda i, j: (i, 0)),
                  pl.BlockSpec((GRID_W, D), lambda i, j: (jnp.minimum(i * hb + hb, nb64 - 1), 0)),
                  g.mod_spec(layer, 0, TM), g.mod_spec(layer, 1, TM),
                  pl.BlockSpec((None, D, tc), lambda i, j: (layer, 0, j)),
                  pl.BlockSpec((None, 9, tc), lambda i, j: (layer, 0, j))],
        out_specs=pl.BlockSpec((TM, tc), lambda i, j: (i, j)),
        scratch_shapes=[pltpu.VMEM((TM + 2 * GRID_W, D), BF16)],
        compiler_params=_cparams(("parallel", "arbitrary")), name="in_proj_conv",
    )(x, x, x, modflat, modflat, w, conv_w)


def _mlstm_kernel(q_ref, k_ref, v_ref, g_ref, gb_ref, inc_ref, eye_ref, y_ref, c_ref, n_ref, m_ref):
    d = pl.program_id(1)
    j = pl.program_id(2)

    @pl.when(j == 0)
    def _():
        c_ref[...] = jnp.zeros_like(c_ref)
        n_ref[...] = jnp.zeros_like(n_ref)
        m_ref[...] = jnp.zeros_like(m_ref)

    L = LCH
    inc = inc_ref[...]
    g = g_ref[...] + gb_ref[...]
    lf = -_softplus(-g)
    b_all = _sel_dot(inc, lf)
    r_all = g - pltpu.roll(b_all, LANE - A_H, 1)
    rt_all = sum(lax.dot_general(eye_ref[...], p, (((1,), (1,)), ((), ())), preferred_element_type=F32)
                 for p in _split3(r_all))
    is_f = d == 0
    valid = inc > 0
    lane = lax.broadcasted_iota(jnp.int32, (1, LANE), 1)
    m_all = m_ref[...]
    hs = range(A_H)
    li = [jnp.where(is_f, g[:, h:h + 1], g[:, 8 + h:9 + h]) for h in hs]
    b = [jnp.where(is_f, b_all[:, 4 + h:5 + h], b_all[:, 12 + h:13 + h]) for h in hs]
    r_row = [jnp.where(is_f, rt_all[h:h + 1, :], rt_all[8 + h:9 + h, :]) for h in hs]
    b_last = [jnp.where(is_f, b_all[L - 1:L, 4 + h:5 + h], b_all[0:1, 12 + h:13 + h]) for h in hs]
    m_prev = [m_all[:, h:h + 1] for h in hs]
    qh = [q_ref[:, h * A_HP:(h + 1) * A_HP] for h in hs]
    kh = [k_ref[:, h * A_HP:(h + 1) * A_HP] * (A_HD ** -0.5) for h in hs]
    vh = [v_ref[:, h * A_HP:(h + 1) * A_HP] for h in hs]
    qk = [_dot_nt(qh[h], kh[h]) for h in hs]
    qc = [_bdot(qh[h], c_ref[h]) for h in hs]
    log_w = [b_last[h] - b[h] + li[h] for h in hs]
    m_new = [jnp.maximum(b_last[h] + m_prev[h], jnp.max(log_w[h], axis=0, keepdims=True)) for h in hs]
    kw = [kh[h] * jnp.exp(log_w[h] - m_new[h]) for h in hs]
    kv = [_dot_tn(kw[h], vh[h]) for h in hs]
    log_d = [jnp.where(valid, b[h] + r_row[h], -jnp.inf) for h in hs]
    m_inter = [b[h] + m_prev[h] for h in hs]
    m_t = [jnp.maximum(m_inter[h], jnp.max(log_d[h], axis=-1, keepdims=True)) for h in hs]
    s = [qk[h] * jnp.exp(log_d[h] - m_t[h]) for h in hs]
    sv = [_bdot(s[h], vh[h]) for h in hs]
    outs = []
    for h in hs:
        w_inter = jnp.exp(m_inter[h] - m_t[h])
        num = sv[h] + w_inter * qc[h]
        den = jnp.sum(s[h], axis=-1, keepdims=True) + w_inter * jnp.sum(qh[h] * n_ref[h], axis=-1, keepdims=True)
        outs.append(num / jnp.maximum(jnp.abs(den), jnp.exp(-m_t[h])))
        cd = jnp.exp(b_last[h] + m_prev[h] - m_new[h])
        c_ref[h] = cd * c_ref[h] + kv[h]
        n_ref[h] = cd * n_ref[h] + jnp.sum(kw[h], axis=0, keepdims=True)
        m_all = jnp.where(lane == h, m_new[h], m_all)
    m_ref[...] = m_all
    y_ref[...] = jnp.concatenate(outs, axis=-1)


def _mlstm_call(g, zc, zr, gate_b, layer, inc):
    blk = g.chunk_block
    qkv = lambda c: pl.BlockSpec((LCH, A_W), lambda b, d, j: (blk(b, d, j), c))
    return pl.pallas_call(
        _mlstm_kernel, out_shape=jax.ShapeDtypeStruct((2, g.rows, A_W), F32),
        grid=(g.batch, 2, g.chunk_steps),
        in_specs=[qkv(ZC_AQ // A_W), qkv(ZC_AK // A_W), qkv(ZC_AV // A_W),
                  pl.BlockSpec((LCH, LANE), lambda b, d, j: (blk(b, d, j), ZR_AG // LANE)),
                  pl.BlockSpec((None, 1, LANE), lambda b, d, j: (layer, 0, 0)),
                  pl.BlockSpec((None, LCH, LCH), lambda b, d, j: (d, 0, 0)),
                  pl.BlockSpec((LANE, LANE), lambda b, d, j: (0, 0))],
        out_specs=pl.BlockSpec((None, LCH, A_W), lambda b, d, j: (d, blk(b, d, j), 0)),
        scratch_shapes=[pltpu.VMEM((A_H, A_HP, A_HP), F32), pltpu.VMEM((A_H, 1, A_HP), F32),
                        pltpu.VMEM((1, LANE), F32)],
        compiler_params=_cparams(("arbitrary", "arbitrary", "arbitrary")), name="mlstm_scan",
    )(zc, zc, zc, zr, gate_b, inc, jnp.eye(LANE, dtype=BF16))


def _gla_kernel(q_ref, k_ref, v_ref, code_ref, gup_ref, gb_ref, w_ref, mask_ref, qm_ref, vm_ref, bd_ref,
                y_ref, st_ref, *, nl):
    j = pl.program_id(2)

    @pl.when(j == 0)
    def _():
        st_ref[...] = jnp.zeros_like(st_ref)

    L = LCH
    x = _bdot(code_ref[...], gup_ref[...]) + gb_ref[...]
    glog = -_softplus(-x) * (1.0 / C_TAU)
    e = _sel_dot(w_ref[...], glog)
    q = q_ref[...] * (C_DK ** -0.5)
    k = k_ref[...]
    v = v_ref[...]
    stack = lambda a: jnp.concatenate([a * qm_ref[h] for h in range(C_H)], axis=0)
    scale = [jnp.exp(e[lv * L:(lv + 1) * L]) for lv in range(nl)]
    prods = [_dot_nt(stack(q * s_), k * s_) for s_ in scale] + [_dot_nt(stack(q), k)]
    att = sum(mask_ref[lv * C_H * L:(lv + 1) * C_H * L, :] * prods[lv] for lv in range(nl + 1))
    o = _dot_nt(q * jnp.exp(e[nl * L:(nl + 1) * L]), st_ref[...])
    av = [_bdot(att[h * L:(h + 1) * L], v) for h in range(C_H)]
    for h in range(C_H):
        o = o + vm_ref[h] * av[h]
    y_ref[...] = o
    ke = k * jnp.exp(e[(nl + 1) * L:(nl + 2) * L])
    tot = e[(nl + 2) * L:(nl + 2) * L + 1]
    st_ref[...] = st_ref[...] * jnp.exp(tot) + bd_ref[...] * _dot_tn(v, ke)


def _gla_call(g, zc, zr, gate_up, gate_b, layer, consts):
    wmat, masks, qmask, vmask, bd, nl = consts
    blk = g.chunk_block
    const = lambda a: pl.BlockSpec(a.shape, lambda b, d, j: (0,) * a.ndim)
    return pl.pallas_call(
        functools.partial(_gla_kernel, nl=nl), out_shape=jax.ShapeDtypeStruct((2, g.rows, C_VW), F32),
        grid=(g.batch, 2, g.chunk_steps),
        in_specs=[pl.BlockSpec((LCH, C_KP), lambda b, d, j: (blk(b, d, j), ZC_CQ // C_KP)),
                  pl.BlockSpec((LCH, C_KP), lambda b, d, j: (blk(b, d, j), ZC_CK // C_KP)),
                  pl.BlockSpec((LCH, C_VW), lambda b, d, j: (blk(b, d, j), ZC_CV // C_VW)),
                  pl.BlockSpec((LCH, LANE), lambda b, d, j: (blk(b, d, j), ZR_CG // LANE)),
                  pl.BlockSpec((None, None, LANE, C_KP), lambda b, d, j: (layer, d, 0, 0)),
                  pl.BlockSpec((None, None, 1, C_KP), lambda b, d, j: (layer, d, 0, 0)),
                  pl.BlockSpec((None,) + wmat.shape[1:], lambda b, d, j: (d, 0, 0)),
                  pl.BlockSpec((None,) + masks.shape[1:], lambda b, d, j: (d, 0, 0)),
                  const(qmask), const(vmask), const(bd)],
        out_specs=pl.BlockSpec((None, LCH, C_VW), lambda b, d, j: (d, blk(b, d, j), 0)),
        scratch_shapes=[pltpu.VMEM((C_VW, C_KP), F32)],
        compiler_params=_cparams(("arbitrary", "arbitrary", "arbitrary")), name="gla_scan",
    )(zc, zc, zc, zr, gate_up, gate_b, wmat, masks, qmask, vmask, bd)


def _rwkv_kernel(r_ref, k_ref, v_ref, wc_ref, ac_ref, w0_ref, wup_ref, a0_ref, aup_ref, kk_ref, ka_ref, rk_ref,
                 incbd_ref, totbd_ref, sm_ref, im_ref, seg_ref, eye_ref, y_ref, bonus_ref,
                 st_ref, lhs_ref, bes_ref, pt_ref, y1_ref, gam_ref):
    d = pl.program_id(1)
    j = pl.program_id(2)

    @pl.when(j == 0)
    def _():
        st_ref[...] = jnp.zeros_like(st_ref)

    r = r_ref[...]
    k = k_ref[...]
    v = v_ref[...]
    seg = seg_ref[...]
    kk = k * kk_ref[...]
    kk = kk / jnp.maximum(jnp.sqrt(_dot_sel(kk * kk, seg)), 1e-12)
    w_log = -_softplus(-(w0_ref[...] + _bdot(jnp.tanh(wc_ref[...]), wup_ref[...]))) - 0.5
    lw = -jnp.exp(w_log)
    a = _sigmoid(a0_ref[...] + _bdot(ac_ref[...], aup_ref[...]))
    kd = k * (1.0 + (a - 1.0) * ka_ref[...])
    be = kk * a
    bonus_ref[...] = _dot_sel(r * kd * rk_ref[...], seg) * v

    cs = _sel_dot(incbd_ref[...], lw)
    tot = _sel_dot(totbd_ref[...], lw)
    e_neg = jnp.exp(-cs)
    e_end = jnp.exp(tot - cs)
    al_s = kk * jnp.exp(cs - lw)
    r_s = r * jnp.exp(cs)
    k_s = kd * e_neg
    b_s = be * e_neg
    ke_s = kd * e_end
    be_s = be * e_end

    L, c, nsub, npair, hd = LCH, RC, LCH // RC, B_H // 2, B_HD
    sm = sm_ref[...]
    im = im_ref[...]
    eye = eye_ref[...]
    lane = lax.broadcasted_iota(jnp.int32, (1, LANE), 1)
    hmask = [(lane < hd).astype(F32), (lane >= hd).astype(F32)]
    for i in range(nsub):
        gam_ref[i] = jnp.exp(tot[i * c:i * c + 1])

    heads = [(p, e) for p in range(npair) for e in range(2)]
    lanes = lambda a, p: a[:, p * LANE:(p + 1) * LANE]
    gg = [_dot_nt(jnp.concatenate([lanes(al_s, p), lanes(r_s, p)], axis=0) * hmask[e],
                  jnp.concatenate([lanes(k_s, p), lanes(b_s, p)], axis=0)) for p, e in heads]
    a_k = [g_[:L, :L] * sm for g_ in gg]
    n_m = [g_[:L, L:] * sm for g_ in gg]
    b_k = [g_[L:, :L] * im for g_ in gg]
    d_m = [g_[L:, L:] * im for g_ in gg]
    tinv = [eye - n_ for n_ in n_m]
    npow = n_m
    for _ in range(int(math.log2(c)) - 1):
        npow = [_bdot(n_, n_) for n_ in npow]
        tinv = [t_ + _bdot(t_, n_) for t_, n_ in zip(tinv, npow)]
    ve = [lanes(v, p) * hmask[e] for p, e in heads]
    at = [_bdot(t_, lanes(al_s, p) * hmask[e]) for t_, (p, e) in zip(tinv, heads)]
    av = [_bdot(a_, v_) for a_, v_ in zip(a_k, ve)]
    w1 = [_bdot(t_, x_) for t_, x_ in zip(tinv, av)]
    rt = [lanes(r_s, p) * hmask[e] - _bdot(d_, a_) for d_, a_, (p, e) in zip(d_m, at, heads)]
    bv = [_bdot(b_, v_) for b_, v_ in zip(b_k, ve)]
    y1 = [x_ - _bdot(d_, w_) for x_, d_, w_ in zip(bv, d_m, w1)]
    for p in range(npair):
        h0, h1 = 2 * p, 2 * p + 1
        vp, kep, bep = lanes(v, p), lanes(ke_s, p), lanes(be_s, p)
        for i in range(nsub):
            rw = slice(i * c, (i + 1) * c)
            lhs_ref[i, p] = jnp.concatenate([at[h0][rw], at[h1][rw], rt[h0][rw], rt[h1][rw]], axis=0).astype(BF16)
            bes = jnp.concatenate([bep[rw] * hmask[0], bep[rw] * hmask[1]], axis=0)
            bes_ref[i, p] = bes.astype(BF16)
            y1_ref[i, p] = jnp.concatenate([y1[h0][rw, :hd], y1[h1][rw, hd:]], axis=0)
            pt_ref[i, p] = _dot_tn(
                jnp.concatenate([vp[rw, :hd], vp[rw, hd:], w1[h0][rw, :hd], w1[h1][rw, hd:]], axis=0),
                jnp.concatenate([kep[rw] * hmask[0], kep[rw] * hmask[1], -bes], axis=0))

    for ii in range(nsub):
        i = jnp.where(d == 0, ii, nsub - 1 - ii)
        gam = gam_ref[i]
        stp = [st_ref[p] for p in range(npair)]
        x = [lax.dot_general(lhs_ref[i, p], stp[p].astype(BF16), (((1,), (1,)), ((), ())),
                             preferred_element_type=F32) for p in range(npair)]
        for p in range(npair):
            st_ref[p] = (stp[p] * gam[:, p * LANE:(p + 1) * LANE] + pt_ref[i, p]
                         - lax.dot_general(x[p][:2 * c].astype(BF16), bes_ref[i, p], (((0,), (0,)), ((), ())),
                                           preferred_element_type=F32))
        ys = []
        for p in range(npair):
            yp = x[p][2 * c:] + y1_ref[i, p]
            ys += [yp[:c], yp[c:]]
        y_ref[pl.ds(pl.multiple_of(i * c, c), c), :] = jnp.concatenate(ys, axis=-1)


def _rwkv_call(g, zc, zr, p, layer, rc):
    blk = g.chunk_block
    nsub, npair = LCH // RC, B_H // 2
    rkv = lambda c: pl.BlockSpec((LCH, B_W), lambda b, d, j: (blk(b, d, j), c))
    code = lambda c: pl.BlockSpec((LCH, LANE), lambda b, d, j: (blk(b, d, j), c))
    per_dir_vec = pl.BlockSpec((None, None, 1, B_W), lambda b, d, j: (layer, d, 0, 0))
    per_dir_mat = pl.BlockSpec((None, None, LANE, B_W), lambda b, d, j: (layer, d, 0, 0))
    vec = pl.BlockSpec((None, 1, B_W), lambda b, d, j: (layer, 0, 0))
    dirc = lambda a: pl.BlockSpec((None,) + a.shape[1:], lambda b, d, j: (d,) + (0,) * (a.ndim - 1))
    const = lambda a: pl.BlockSpec(a.shape, lambda b, d, j: (0,) * a.ndim)
    out = pl.BlockSpec((None, LCH, B_W), lambda b, d, j: (d, blk(b, d, j), 0))
    return pl.pallas_call(
        _rwkv_kernel,
        out_shape=(jax.ShapeDtypeStruct((2, g.rows, B_W), F32), jax.ShapeDtypeStruct((2, g.rows, B_W), F32)),
        grid=(g.batch, 2, g.chunk_steps),
        in_specs=[rkv(ZC_BR // B_W), rkv(ZC_BK // B_W), rkv(ZC_BV // B_W),
                  code(ZR_BW // LANE), code(ZR_BA // LANE),
                  per_dir_vec, per_dir_mat, per_dir_vec, per_dir_mat, vec, vec, vec,
                  dirc(rc["incbd"]), const(rc["totbd"]), dirc(rc["sm"]), dirc(rc["im"]), const(rc["seg"]),
                  const(rc["eye"])],
        out_specs=(out, out),
        scratch_shapes=[pltpu.VMEM((npair, B_HD, LANE), F32), pltpu.VMEM((nsub, npair, 4 * RC, LANE), BF16),
                        pltpu.VMEM((nsub, npair, 2 * RC, LANE), BF16), pltpu.VMEM((nsub, npair, B_HD, LANE), F32),
                        pltpu.VMEM((nsub, npair, 2 * RC, B_HD), F32), pltpu.VMEM((nsub, 1, B_W), F32)],
        compiler_params=_cparams(("arbitrary", "arbitrary", "arbitrary")), name="rwkv_scan",
    )(zc, zc, zc, zr, zr, p["w0"], p["w_up"], p["a0"], p["a_up"], p["kk"], p["ka"], p["rk"],
      rc["incbd"], rc["totbd"], rc["sm"], rc["im"], rc["seg"], rc["eye"])


def _post_kernel(ya_ref, yb_ref, bn_ref, yc_ref, ao_ref, bg_ref, co_ref, gna_ref, gnb_ref, gup_ref, gnc_ref,
                 sega_ref, segb_ref, segc_ref, o_ref):
    def seg_sum(x, seg):
        w = seg.shape[0]
        return jnp.concatenate([_dot_sel(x[:, i * w:(i + 1) * w], seg) for i in range(x.shape[1] // w)], axis=-1)

    ha = ya_ref[0] + ya_ref[1]
    sega = sega_ref[...]
    hf = ha - seg_sum(ha, sega) * (1.0 / A_HD)
    hn = hf * lax.rsqrt(seg_sum(hf * hf, sega) * (1.0 / A_HD) + HEAD_EPS)
    h_a = _sigmoid(ao_ref[...]) * (hn * gna_ref[...])

    yb = yb_ref[0] + yb_ref[1]
    segb = segb_ref[...]
    yf = yb - seg_sum(yb, segb) * (1.0 / B_HD)
    yn = yf * lax.rsqrt(seg_sum(yf * yf, segb) * (1.0 / B_HD) + B_GN_EPS)
    ob = yn * gnb_ref[0] + gnb_ref[1] + bn_ref[0] + bn_ref[1]
    h_b = ob * _bdot(_sigmoid(bg_ref[...]), gup_ref[...])

    oc = yc_ref[0] + yc_ref[1]
    on = oc * lax.rsqrt(_dot_sel(oc * oc, segc_ref[...]) * (1.0 / C_DV) + HEAD_EPS) * gnc_ref[...]
    co = co_ref[...]
    h_c = on * (co * _sigmoid(co))
    o_ref[...] = jnp.concatenate([h_a, h_b, h_c], axis=-1).astype(BF16)


def _post_call(g, ya, yb, bonus, yc, zr, p, layer, segs):
    tm = TM_S
    two = lambda w: pl.BlockSpec((2, tm, w), lambda i: (0, i, 0))
    zcol = lambda w, off: pl.BlockSpec((tm, w), lambda i: (i, off // w))
    const = lambda a: pl.BlockSpec(a.shape, lambda i: (0,) * a.ndim)
    return pl.pallas_call(
        _post_kernel, out_shape=jax.ShapeDtypeStruct((g.rows, H_W), BF16), grid=(g.rows // tm,),
        in_specs=[two(A_W), two(B_W), two(B_W), two(C_VW),
                  zcol(A_W, ZR_AO), zcol(LANE, ZR_BG), zcol(C_VW, ZR_CO),
                  pl.BlockSpec((None, 1, A_W), lambda i: (layer, 0, 0)),
                  pl.BlockSpec((None, 2, 1, B_W), lambda i: (layer, 0, 0, 0)),
                  pl.BlockSpec((None, LANE, B_W), lambda i: (layer, 0, 0)),
                  pl.BlockSpec((None, 1, C_VW), lambda i: (layer, 0, 0)),
                  const(segs[0]), const(segs[1]), const(segs[2])],
        out_specs=pl.BlockSpec((tm, H_W), lambda i: (i, 0)),
        compiler_params=_cparams(("parallel",)), name="mixer_post",
    )(ya, yb, bonus, yc, zr, zr, zr, p["gn_a"], p["gn_b"], p["g_up"], p["gn_c"], *segs)


def _merge_kernel(h_ref, mg_ref, x_ref, gt_ref, lng_ref, lnb_ref, wcat_ref, wout_ref, o_ref, *, alpha):
    h = h_ref[...]
    mg = mg_ref[...]
    o1, o2 = A_W, A_W + B_W
    merged = (_sigmoid(mg[:, :D]) * jnp.dot(h[:, :o1], wcat_ref[:o1], preferred_element_type=F32)
              + _sigmoid(mg[:, D:2 * D]) * jnp.dot(h[:, o1:o2], wcat_ref[o1:o2], preferred_element_type=F32)
              + _sigmoid(mg[:, 2 * D:]) * jnp.dot(h[:, o2:], wcat_ref[o2:], preferred_element_type=F32))
    y = jnp.dot(merged.astype(BF16), wout_ref[...], preferred_element_type=F32)
    o_ref[...] = _ln(alpha * x_ref[...] + gt_ref[...] * y) * lng_ref[...] + lnb_ref[...]


def _ln_spec(layer, sub, which):
    return pl.BlockSpec((None, None, None, 1, D), lambda i, *_: (layer, sub, which, 0, 0))


def _merge_call(g, h, zr, x, modflat, ln_post, wcat, wout, layer, alpha):
    tm = TM_S
    return pl.pallas_call(
        functools.partial(_merge_kernel, alpha=alpha), out_shape=jax.ShapeDtypeStruct((g.rows, D), F32),
        grid=(g.rows // tm,),
        in_specs=[pl.BlockSpec((tm, H_W), lambda i: (i, 0)),
                  pl.BlockSpec((tm, 3 * D), lambda i: (i, 0)),
                  pl.BlockSpec((tm, D), lambda i: (i, 0)),
                  g.mod_spec(layer, 2, tm), _ln_spec(layer, 0, 0), _ln_spec(layer, 0, 1),
                  pl.BlockSpec((None, H_W, D), lambda i: (layer, 0, 0), pipeline_mode=pl.Buffered(1)),
                  pl.BlockSpec((None, D, D), lambda i: (layer, 0, 0), pipeline_mode=pl.Buffered(1))],
        out_specs=pl.BlockSpec((tm, D), lambda i: (i, 0)),
        compiler_params=_cparams(("parallel",)), name="branch_merge",
    )(h, zr, x, modflat, ln_post, ln_post, wcat, wout)


def _ffn_kernel(x_ref, sh_ref, sc_ref, gt_ref, lng_ref, lnb_ref, wg_ref, wu_ref, wd_ref, o_ref, u_ref, acc_ref,
                *, alpha):
    j = pl.program_id(1)

    @pl.when(j == 0)
    def _():
        u_ref[...] = (_ln(x_ref[...]) * (1.0 + sc_ref[...]) + sh_ref[...]).astype(BF16)
        acc_ref[...] = jnp.zeros_like(acc_ref)

    acc_ref[...] += _swiglu_tile(u_ref[...], wg_ref[0], wu_ref[0], wd_ref[0], j)

    @pl.when(j == pl.num_programs(1) - 1)
    def _():
        o_ref[...] = _ln(alpha * x_ref[...] + gt_ref[...] * acc_ref[...]) * lng_ref[...] + lnb_ref[...]


def _ff_start(j):
    return pl.multiple_of(jnp.minimum(j * TF, D_FF - TF), LANE)


def _swiglu_tile(u, wg, wu, wd, j):
    gate = jnp.dot(u, wg, preferred_element_type=F32)
    up = jnp.dot(u, wu, preferred_element_type=F32)
    col = _ff_start(j) + lax.broadcasted_iota(jnp.int32, (1, TF), 1)
    hid = jnp.where(col >= j * TF, (gate * _sigmoid(gate)) * up, 0.0)
    return jnp.dot(hid.astype(BF16), wd, preferred_element_type=F32)


def _ffn_call(g, x, modflat, ln_post, wg, wu, wd, layer, idx, alpha, out_rows):
    el = pl.Element
    return pl.pallas_call(
        functools.partial(_ffn_kernel, alpha=alpha), out_shape=jax.ShapeDtypeStruct((out_rows, D), F32),
        grid=(out_rows // TM, N_FF_TILES),
        in_specs=[pl.BlockSpec((TM, D), lambda i, j: (i, 0)),
                  g.mod_spec(layer, 3, TM), g.mod_spec(layer, 4, TM), g.mod_spec(layer, 5, TM),
                  _ln_spec(layer, 1, 0), _ln_spec(layer, 1, 1),
                  pl.BlockSpec((el(1), el(D), el(TF)), lambda i, j: (idx, 0, _ff_start(j))),
                  pl.BlockSpec((el(1), el(D), el(TF)), lambda i, j: (idx, 0, _ff_start(j))),
                  pl.BlockSpec((el(1), el(TF), el(D)), lambda i, j: (idx, _ff_start(j), 0))],
        out_specs=pl.BlockSpec((TM, D), lambda i, j: (i, 0)),
        scratch_shapes=[pltpu.VMEM((TM, D), BF16), pltpu.VMEM((TM, D), F32)],
        compiler_params=_cparams(("parallel", "arbitrary")), name="ffn_swiglu",
    )(x, modflat, modflat, modflat, ln_post, ln_post, wg, wu, wd)


def _router_kernel(x_ref, sh_ref, sc_ref, w_ref, b_ref, u_ref, sel_ref):
    u = _ln(x_ref[...]) * (1.0 + sc_ref[...]) + sh_ref[...]
    u_ref[...] = u
    lane = lax.broadcasted_iota(jnp.int32, (1, LANE), 1)
    logits = jnp.where(lane < N_EXPERTS, _dot3(u, w_ref[...]) + b_ref[...], -jnp.inf)
    m1 = jnp.max(logits, axis=-1, keepdims=True)
    i1 = jnp.min(jnp.where(logits == m1, lane, LANE), axis=-1, keepdims=True)
    rest = jnp.where(lane == i1, -jnp.inf, logits)
    m2 = jnp.max(rest, axis=-1, keepdims=True)
    i2 = jnp.min(jnp.where(rest == m2, lane, LANE), axis=-1, keepdims=True)
    e2 = jnp.exp(m2 - m1)
    p1 = 1.0 / (1.0 + e2)
    sel_ref[...] = (jnp.where(lane == 0, i1.astype(F32), 0.0) + jnp.where(lane == 1, i2.astype(F32), 0.0)
                    + jnp.where(lane == 2, p1, 0.0) + jnp.where(lane == 3, e2 * p1, 0.0))


def _router_call(g, x, modflat, router, router_b, layer, idx):
    return pl.pallas_call(
        _router_kernel,
        out_shape=(jax.ShapeDtypeStruct((g.rows, D), F32), jax.ShapeDtypeStruct((g.rows, LANE), F32)),
        grid=(g.rows // TM,),
        in_specs=[pl.BlockSpec((TM, D), lambda i: (i, 0)), g.mod_spec(layer, 3, TM), g.mod_spec(layer, 4, TM),
                  pl.BlockSpec((None, D, LANE), lambda i: (idx, 0, 0)),
                  pl.BlockSpec((None, 1, LANE), lambda i: (idx, 0, 0))],
        out_specs=(pl.BlockSpec((TM, D), lambda i: (i, 0)), pl.BlockSpec((TM, LANE), lambda i: (i, 0))),
        compiler_params=_cparams(("parallel",)), name="moe_router",
    )(x, modflat, modflat, router, router_b)


def _route(sel, n_slots):
    t = sel.shape[0]
    expert = jnp.concatenate([sel[:, 0], sel[:, 1]]).astype(jnp.int32)
    prob = jnp.concatenate([sel[:, 2], sel[:, 3]])
    onehot = (expert[:, None] == jnp.arange(N_EXPERTS, dtype=jnp.int32)[None, :]).astype(jnp.int32)
    csum = jnp.cumsum(onehot, axis=0)
    rank = jnp.sum(csum * onehot, axis=1) - 1
    count = csum[-1]
    padded = ((count + TM - 1) // TM) * TM
    group_end = jnp.cumsum(padded)
    pos = jnp.sum(onehot * (group_end - padded)[None, :], axis=1) + rank
    token = jnp.concatenate([jnp.arange(t, dtype=F32)] * 2)
    slots = jnp.zeros((n_slots, 2), F32).at[pos].set(jnp.stack([token, prob], axis=1))
    src, prob_sorted = slots[:, 0].astype(jnp.int32), slots[:, 1]
    tile_start = jnp.arange(n_slots // TM, dtype=jnp.int32) * TM
    tile_expert = jnp.sum((tile_start[:, None] >= group_end[None, :]).astype(jnp.int32), axis=1)
    tile_valid = (tile_expert < N_EXPERTS).astype(jnp.int32)
    return src, prob_sorted, jnp.minimum(tile_expert, N_EXPERTS - 1), tile_valid, pos


def _row_gather_kernel(idx_ref, src_ref, o_ref, sem, *, n):
    def row_copy(r):
        return pltpu.make_async_copy(src_ref.at[pl.ds(idx_ref[0, 0, r], 1)], o_ref.at[pl.ds(r, 1)], sem)

    def start(r, carry):
        row_copy(r).start()
        return carry

    def wait(r, carry):
        row_copy(r).wait()
        return carry

    lax.fori_loop(0, n, start, 0, unroll=8)
    lax.fori_loop(0, n, wait, 0, unroll=8)


def _row_gather_call(src, idx):
    n = idx.shape[0]
    return pl.pallas_call(
        functools.partial(_row_gather_kernel, n=TM), out_shape=jax.ShapeDtypeStruct((n, D), F32),
        grid=(n // TM,),
        in_specs=[pl.BlockSpec((1, 1, TM), lambda i: (i, 0, 0), memory_space=pltpu.SMEM),
                  pl.BlockSpec(memory_space=pl.ANY)],
        out_specs=pl.BlockSpec((TM, D), lambda i: (i, 0)),
        scratch_shapes=[pltpu.SemaphoreType.DMA(())],
        compiler_params=_cparams(("arbitrary",)), name="moe_gather",
    )(idx.reshape(n // TM, 1, TM), src)


def _moe_group_kernel(te_ref, valid_ref, xs_ref, pr_ref, wg_ref, wu_ref, wd_ref, o_ref, u_ref, acc_ref):
    i = pl.program_id(0)
    j = pl.program_id(1)

    @pl.when(j == 0)
    def _():
        u_ref[...] = xs_ref[...].astype(BF16)
        acc_ref[...] = jnp.zeros_like(acc_ref)

    @pl.when(valid_ref[i] > 0)
    def _():
        acc_ref[...] += _swiglu_tile(u_ref[...], wg_ref[0, 0], wu_ref[0, 0], wd_ref[0, 0], j)

    @pl.when(j == pl.num_programs(1) - 1)
    def _():
        o_ref[...] = pr_ref[...] * acc_ref[...]


def _moe_group_call(xs, prob_sorted, tile_expert, tile_valid, wg, wu, wd, idx):
    n = xs.shape[0]
    el = pl.Element
    fcol = lambda i, j, te, va: _ff_start(jnp.where(va[i] > 0, j, 0))
    grid_spec = pltpu.PrefetchScalarGridSpec(
        num_scalar_prefetch=2, grid=(n // TM, N_FF_TILES),
        in_specs=[pl.BlockSpec((TM, D), lambda i, j, te, va: (i, 0)),
                  pl.BlockSpec((TM, 1), lambda i, j, te, va: (i, 0)),
                  pl.BlockSpec((el(1), el(1), el(D), el(TF)),
                               lambda i, j, te, va: (idx, te[i], 0, fcol(i, j, te, va))),
                  pl.BlockSpec((el(1), el(1), el(D), el(TF)),
                               lambda i, j, te, va: (idx, te[i], 0, fcol(i, j, te, va))),
                  pl.BlockSpec((el(1), el(1), el(TF), el(D)),
                               lambda i, j, te, va: (idx, te[i], fcol(i, j, te, va), 0))],
        out_specs=pl.BlockSpec((TM, D), lambda i, j, te, va: (i, 0)),
        scratch_shapes=[pltpu.VMEM((TM, D), BF16), pltpu.VMEM((TM, D), F32)])
    return pl.pallas_call(
        _moe_group_kernel, out_shape=jax.ShapeDtypeStruct((n, D), F32), grid_spec=grid_spec,
        compiler_params=_cparams(("arbitrary", "arbitrary")), name="moe_experts",
    )(tile_expert, tile_valid, xs, prob_sorted.reshape(n, 1), wg, wu, wd)


def _moe_combine_kernel(pos_ref, ys_ref, x_ref, gt_ref, lng_ref, lnb_ref, o_ref, buf_ref, sem, *, alpha):
    def row_copy(r):
        return pltpu.make_async_copy(ys_ref.at[pl.ds(pos_ref[0, 0, r], 1)], buf_ref.at[pl.ds(r, 1)], sem)

    def start(r, carry):
        row_copy(r).start()
        return carry

    def wait(r, carry):
        row_copy(r).wait()
        return carry

    lax.fori_loop(0, 2 * TM, start, 0, unroll=8)
    lax.fori_loop(0, 2 * TM, wait, 0, unroll=8)
    f = buf_ref[:TM, :] + buf_ref[TM:, :]
    o_ref[...] = _ln(alpha * x_ref[...] + gt_ref[...] * f) * lng_ref[...] + lnb_ref[...]


def _moe_combine_call(g, ys, pos, x, modflat, ln_post, layer, alpha, out_rows):
    nt = g.rows // TM
    pos2 = jnp.concatenate([pos[:g.rows].reshape(nt, 1, TM), pos[g.rows:].reshape(nt, 1, TM)], axis=-1)
    return pl.pallas_call(
        functools.partial(_moe_combine_kernel, alpha=alpha), out_shape=jax.ShapeDtypeStruct((out_rows, D), F32),
        grid=(out_rows // TM,),
        in_specs=[pl.BlockSpec((1, 1, 2 * TM), lambda i: (i, 0, 0), memory_space=pltpu.SMEM),
                  pl.BlockSpec(memory_space=pl.ANY),
                  pl.BlockSpec((TM, D), lambda i: (i, 0)),
                  g.mod_spec(layer, 5, TM), _ln_spec(layer, 1, 0), _ln_spec(layer, 1, 1)],
        out_specs=pl.BlockSpec((TM, D), lambda i: (i, 0)),
        scratch_shapes=[pltpu.VMEM((2 * TM, D), F32), pltpu.SemaphoreType.DMA(())],
        compiler_params=_cparams(("arbitrary",)), name="moe_combine",
    )(pos2, ys, x, modflat, ln_post, ln_post)


def kernel(x, c, ctx, c_ctx, w_mod, b_mod, w_in, conv_k, mlstm_gate_b, mlstm_gn, rwkv_w0, rwkv_w_up, rwkv_a0,
           rwkv_a_up, rwkv_g_up, rwkv_kk, rwkv_ka, rwkv_rk, rwkv_gn, gla_gate_up, gla_gate_b, gla_gn, w_branch_a,
           w_branch_b, w_branch_c, w_out, ln_post, ffn_w_gate, ffn_w_up, ffn_w_down, moe_router, moe_router_b,
           moe_w_gate, moe_w_up, moe_w_down):
    batch, seq, _ = x.shape
    ctx_len = ctx.shape[1]
    depth = w_mod.shape[0]
    alpha = (2.0 * depth) ** 0.25
    g = _Geom(batch, seq, ctx_len)

    n_conv = 5504
    w_conv = _pack_conv_cols(w_in[..., :n_conv].astype(BF16))
    w_rest = _pack_rest_cols(w_in[..., n_conv:].astype(BF16))
    conv_w = _pack_conv_cols(conv_k.reshape(depth, 9, n_conv))
    gate_b = _pad_last(mlstm_gate_b.reshape(depth, 1, 16), LANE)

    def rows_at(a, per, n):
        out = jnp.zeros((depth, 2, n, a.shape[-1]), a.dtype)
        for d in range(2):
            out = out.at[:, d, d * per:(d + 1) * per].set(a[:, d])
        return out

    rw = dict(w0=rwkv_w0.reshape(depth, 2, 1, B_W), w_up=rows_at(rwkv_w_up, B_W_LORA, LANE).astype(BF16),
              a0=rwkv_a0.reshape(depth, 2, 1, B_W), a_up=rows_at(rwkv_a_up, B_A_LORA, LANE).astype(BF16),
              kk=rwkv_kk.reshape(depth, 1, B_W), ka=rwkv_ka.reshape(depth, 1, B_W),
              rk=rwkv_rk.reshape(depth, 1, B_W))
    gla_up = _pad_last(rows_at(gla_gate_up, C_G_LORA, LANE), C_KP).astype(BF16)
    gla_b = _pad_last(gla_gate_b.reshape(depth, 2, 1, C_KW), C_KP)
    post_p = dict(gn_a=_pad_heads(mlstm_gn, A_H, A_HD, A_HP).reshape(depth, 1, A_W),
                  gn_b=rwkv_gn.reshape(depth, 2, 1, B_W), g_up=rwkv_g_up.astype(BF16),
                  gn_c=gla_gn.reshape(depth, 1, C_VW))
    wa = _pad_heads(jnp.swapaxes(w_branch_a, 1, 2), A_H, A_HD, A_HP)
    wcat = jnp.concatenate([jnp.swapaxes(wa, 1, 2), w_branch_b, w_branch_c], axis=1).astype(BF16)
    wout = w_out.astype(BF16)
    lnp = ln_post.reshape(depth, 2, 2, 1, D)
    f_wg, f_wu, f_wd = ffn_w_gate.astype(BF16), ffn_w_up.astype(BF16), ffn_w_down.astype(BF16)
    m_wg, m_wu, m_wd = moe_w_gate.astype(BF16), moe_w_up.astype(BF16), moe_w_down.astype(BF16)
    router = _pad_last(moe_router, LANE)
    router_b = _pad_last(moe_router_b.reshape(-1, 1, N_EXPERTS), LANE)

    inc_a = _mlstm_consts()
    gla_c = _gla_consts()
    rwkv_c = _rwkv_consts()
    segs = _seg_consts()

    cvec = jnp.concatenate([c, c_ctx[None, :], jnp.zeros((8 - batch - 1, D), F32)], axis=0)
    modflat = _mod_call(cvec, w_mod, b_mod).reshape(depth * 8, 1, 6 * D)

    xs = jnp.concatenate([x.reshape(batch * seq, D), ctx.reshape(batch * ctx_len, D)], axis=0)
    for l in range(depth):
        zc = _inproj_conv_call(g, xs, modflat, l, w_conv, conv_w)
        zr = _inproj_call(g, xs, modflat, l, w_rest, 1408, "in_proj_rest")
        ya = _mlstm_call(g, zc, zr, gate_b, l, inc_a)
        yb, bonus = _rwkv_call(g, zc, zr, rw, l, rwkv_c)
        yc = _gla_call(g, zc, zr, gla_up, gla_b, l, gla_c)
        h = _post_call(g, ya, yb, bonus, yc, zr, post_p, l, segs)
        xs = _merge_call(g, h, zr, xs, modflat, lnp, wcat, wout, l, alpha)
        i = l // 2
        out_rows = g.rows if l < depth - 1 else g.n_lat_rows
        if l % 2 == 0:
            xs = _ffn_call(g, xs, modflat, lnp, f_wg, f_wu, f_wd, l, i, alpha, out_rows)
        else:
            u, sel = _router_call(g, xs, modflat, router, router_b, l, i)
            n_slots = -(-(2 * g.rows + N_EXPERTS * (TM - 1)) // TM) * TM
            src, prob_sorted, tile_expert, tile_valid, pos = _route(sel, n_slots)
            ys = _moe_group_call(_row_gather_call(u, src), prob_sorted, tile_expert, tile_valid, m_wg, m_wu, m_wd, i)
            xs = _moe_combine_call(g, ys, pos, xs, modflat, lnp, l, alpha, out_rows)
    return xs.reshape(batch, seq, D)
```

```python
import functools
import math

import jax
import jax.numpy as jnp
import numpy as np
from jax import lax
from jax.experimental import pallas as pl
from jax.experimental.pallas import tpu as pltpu

F32 = jnp.float32
BF16 = jnp.bfloat16

D = 2048
GRID_W = 64
A_H, A_HD, A_HP = 4, 192, 256
A_W = A_H * A_HP
B_H, B_HD = 10, 64
B_W = B_H * B_HD
B_W_LORA, B_A_LORA, B_G_LORA = 64, 32, 128
B_GN_EPS = 64e-5
C_H, C_DK, C_DV = 4, 80, 160
C_KW, C_KP, C_VW = C_H * C_DK, 384, C_H * C_DV
C_G_LORA = 16
C_TAU = 16.0
D_FF = 5504
TF = 512
N_FF_TILES = -(-D_FF // TF)
N_EXPERTS = 8
LN_EPS = 1e-6
HEAD_EPS = 1e-5
LANE = 128

ZC_AQ, ZC_AK, ZC_AV = 0, 1024, 2048
ZC_BR, ZC_BK, ZC_BV, ZC_CV = 3200, 3840, 4480, 5120
ZC_CQ, ZC_CK = 5760, 6144
ZC_W = 6656
ZR_MG, ZR_AO, ZR_AG, ZR_BW, ZR_BA, ZR_BG, ZR_CO, ZR_CG = 0, 6144, 7168, 7296, 7424, 7552, 7680, 8320
ZR_W = 8448
H_W = A_W + B_W + C_VW

TM = 512
TM_S = 256
LCH = 128
RC = 64
VMEM_LIMIT = 56 * 1024 * 1024


def _cparams(sem):
    return pltpu.CompilerParams(dimension_semantics=sem, vmem_limit_bytes=VMEM_LIMIT)


def _ln(a):
    mu = jnp.mean(a, axis=-1, keepdims=True)
    d = a - mu
    var = jnp.mean(d * d, axis=-1, keepdims=True)
    return d * lax.rsqrt(var + LN_EPS)


def _sigmoid(x):
    return 1.0 / (1.0 + jnp.exp(-x))


def _softplus(x):
    return jnp.maximum(x, 0.0) + jnp.log(1.0 + jnp.exp(-jnp.abs(x)))


def _bdot(a, b):
    return jnp.dot(a.astype(BF16), b.astype(BF16), preferred_element_type=F32)


def _dot_nt(a, b):
    return lax.dot_general(a.astype(BF16), b.astype(BF16), (((1,), (1,)), ((), ())), preferred_element_type=F32)


def _dot_tn(a, b):
    return lax.dot_general(a.astype(BF16), b.astype(BF16), (((0,), (0,)), ((), ())), preferred_element_type=F32)


def _split2(x):
    x1 = x.astype(BF16)
    return x1, (x - x1.astype(F32)).astype(BF16)


def _split3(x):
    x1 = x.astype(BF16)
    r = x - x1.astype(F32)
    x2 = r.astype(BF16)
    return x1, x2, (r - x2.astype(F32)).astype(BF16)


def _sel_dot(m, x):
    return sum(jnp.dot(m, p, preferred_element_type=F32) for p in _split3(x))


def _dot_sel(x, m):
    return sum(jnp.dot(p, m, preferred_element_type=F32) for p in _split2(x))


def _dot3(a, b):
    a1, a2 = _split2(a)
    b1, b2 = _split2(b)
    return (jnp.dot(a1, b1, preferred_element_type=F32) + jnp.dot(a1, b2, preferred_element_type=F32)
            + jnp.dot(a2, b1, preferred_element_type=F32))


def _dir_pos(n):
    return np.stack([np.arange(n), n - 1 - np.arange(n)])


def _mlstm_consts():
    pos = _dir_pos(LCH)
    inc = (pos[:, None, :] <= pos[:, :, None]).astype(np.float32)
    return jnp.asarray(inc, BF16)


def _gla_consts():
    L = LCH
    pos = _dir_pos(L)
    widths = [L >> i for i in range(int(math.log2(L)))]
    nl = len(widths)
    wmat = np.zeros((2, (nl + 2) * L + 8, L), np.float32)
    masks = np.zeros((2, (nl + 1) * C_H * L, L), np.float32)
    for d in range(2):
        p = pos[d]
        for li, w in enumerate(widths):
            half = w // 2
            blk, off = p // w, p % w
            mid = blk * w + half
            isq, isk = off >= half, off < half
            eq = (p[None, :] >= mid[:, None]) & (p[None, :] <= p[:, None]) & isq[:, None]
            ek = (p[None, :] > p[:, None]) & (p[None, :] <= mid[:, None] - 1) & isk[:, None]
            wmat[d, li * L:(li + 1) * L] = eq | ek
            m = (blk[:, None] == blk[None, :]) & isq[:, None] & isk[None, :]
            masks[d, li * C_H * L:(li + 1) * C_H * L] = np.tile(m, (C_H, 1))
        wmat[d, nl * L:(nl + 1) * L] = p[None, :] <= p[:, None]
        wmat[d, (nl + 1) * L:(nl + 2) * L] = p[None, :] > p[:, None]
        wmat[d, (nl + 2) * L:] = 1.0
        masks[d, nl * C_H * L:] = np.tile(np.eye(L), (C_H, 1))
    qmask = np.zeros((C_H, 1, C_KP), np.float32)
    vmask = np.zeros((C_H, 1, C_VW), np.float32)
    bd = np.zeros((C_VW, C_KP), np.float32)
    for h in range(C_H):
        qmask[h, 0, h * C_DK:(h + 1) * C_DK] = 1.0
        vmask[h, 0, h * C_DV:(h + 1) * C_DV] = 1.0
        bd[h * C_DV:(h + 1) * C_DV, h * C_DK:(h + 1) * C_DK] = 1.0
    return (jnp.asarray(wmat, BF16), jnp.asarray(masks, F32), jnp.asarray(qmask, F32), jnp.asarray(vmask, F32),
            jnp.asarray(bd, F32), nl)


def _rwkv_consts():
    c, H, L = RC, B_H, LCH
    pos = _dir_pos(c)
    inc = (pos[:, None, :] <= pos[:, :, None]).astype(np.float32)
    nsub = L // c
    incbd = np.zeros((2, L, L), np.float32)
    for d in range(2):
        for i in range(nsub):
            incbd[d, i * c:(i + 1) * c, i * c:(i + 1) * c] = inc[d]
    sub = np.arange(L) // c
    same = sub[:, None] == sub[None, :]
    totbd = same.astype(np.float32)
    pp = np.stack([np.tile(pos[d], nsub) for d in range(2)])
    sm = (same[None] & (pp[:, None, :] < pp[:, :, None])).astype(np.float32)
    im = (same[None] & (pp[:, None, :] <= pp[:, :, None])).astype(np.float32)
    lane_head = np.repeat(np.arange(H), B_HD)
    seg = (lane_head[:, None] == lane_head[None, :]).astype(np.float32)
    return dict(incbd=jnp.asarray(incbd, BF16), totbd=jnp.asarray(totbd, BF16), sm=jnp.asarray(sm, F32),
                im=jnp.asarray(im, F32), seg=jnp.asarray(seg, BF16), eye=jnp.eye(L, dtype=F32))


def _seg_consts():
    va = np.arange(A_HP) < A_HD
    sega = va[:, None] & va[None, :]
    lb = np.arange(LANE) // B_HD
    segb = lb[:, None] == lb[None, :]
    lc = np.arange(C_VW) // C_DV
    segc = lc[:, None] == lc[None, :]
    return jnp.asarray(sega, BF16), jnp.asarray(segb, BF16), jnp.asarray(segc, BF16)


def _pad_last(a, n):
    return jnp.pad(a, [(0, 0)] * (a.ndim - 1) + [(0, n - a.shape[-1])])


def _pad_heads(a, nh, hd, hp):
    a = a.reshape(a.shape[:-1] + (nh, hd))
    a = jnp.pad(a, [(0, 0)] * (a.ndim - 1) + [(0, hp - hd)])
    return a.reshape(a.shape[:-2] + (nh * hp,))


def _pack_conv_cols(w):
    s = lambda a, b: w[..., a:b]
    z = lambda n: jnp.zeros(w.shape[:-1] + (n,), w.dtype)
    parts = [_pad_heads(s(0, 768), A_H, A_HD, A_HP), _pad_heads(s(768, 1536), A_H, A_HD, A_HP),
             _pad_heads(s(1536, 2304), A_H, A_HD, A_HP), z(128),
             s(2304, 2944), s(2944, 3584), s(3584, 4224), s(4864, 5504),
             _pad_last(s(4224, 4544), C_KP), _pad_last(s(4544, 4864), C_KP), z(128)]
    return jnp.concatenate(parts, axis=-1)


def _pack_rest_cols(w):
    s = lambda a, b: w[..., a:b]
    parts = [s(0, 6144), _pad_heads(s(6144, 6912), A_H, A_HD, A_HP), _pad_last(s(6912, 6928), LANE),
             s(6928, 7056), _pad_last(s(7056, 7120), LANE), s(7120, 7248), s(7280, 7920),
             _pad_last(s(7248, 7280), LANE)]
    return jnp.concatenate(parts, axis=-1)


def _mod_kernel(c_ref, w_ref, b_ref, o_ref):
    c = c_ref[...]
    s = c * _sigmoid(c)
    o_ref[...] = jnp.dot(s, w_ref[...], preferred_element_type=F32) + b_ref[...]


def _mod_call(cvec, w_mod, b_mod):
    depth, _, n6 = w_mod.shape
    tn = 1536
    return pl.pallas_call(
        _mod_kernel, out_shape=jax.ShapeDtypeStruct((depth, 8, n6), F32), grid=(depth, n6 // tn),
        in_specs=[pl.BlockSpec((8, D), lambda l, j: (0, 0)),
                  pl.BlockSpec((None, D, tn), lambda l, j: (l, 0, j)),
                  pl.BlockSpec((None, 1, tn), lambda l, j: (l, 0, j))],
        out_specs=pl.BlockSpec((None, 8, tn), lambda l, j: (l, 0, j)),
        compiler_params=_cparams(("parallel", "parallel")), name="mod_vectors",
    )(cvec, w_mod, b_mod.reshape(depth, 1, n6))


class _Geom:
    def __init__(self, batch, seq, ctx_len):
        assert seq % TM == 0 and (batch * ctx_len) % TM == 0 and TM % ctx_len == 0 and TM % GRID_W == 0
        assert ctx_len % LCH == 0 and seq % LCH == 0 and ctx_len & (ctx_len - 1) == 0
        self.batch, self.seq, self.ctx_len = batch, seq, ctx_len
        self.rows = batch * (seq + ctx_len)
        self.n_lat_rows = batch * seq

    def mod_row(self, i, tm):
        return jnp.minimum(i // (self.seq // tm), self.batch)

    def mod_spec(self, layer, k, tm):
        return pl.BlockSpec((None, 1, D), lambda i, *_: (layer * 8 + self.mod_row(i, tm), 0, k))

    def chunk_block(self, b, d, j):
        ncc, nlc = self.ctx_len // LCH, self.seq // LCH
        is_ctx = j < ncc
        jj = jnp.where(is_ctx, j, j - ncc)
        n = jnp.where(is_ctx, ncc, nlc)
        loc = jnp.where(d == 0, jj, n - 1 - jj)
        return jnp.where(is_ctx, self.n_lat_rows // LCH + b * ncc + loc, b * nlc + loc)

    @property
    def chunk_steps(self):
        return (self.ctx_len + self.seq) // LCH


def _inproj_kernel(x_ref, sh_ref, sc_ref, w_ref, o_ref, u_ref):
    @pl.when(pl.program_id(1) == 0)
    def _():
        u_ref[...] = (_ln(x_ref[...]) * (1.0 + sc_ref[...]) + sh_ref[...]).astype(BF16)

    o_ref[...] = jnp.dot(u_ref[...], w_ref[...], preferred_element_type=F32)


def _inproj_call(g, x, modflat, layer, w, tn, name):
    n = w.shape[-1]
    return pl.pallas_call(
        _inproj_kernel, out_shape=jax.ShapeDtypeStruct((g.rows, n), F32), grid=(g.rows // TM, n // tn),
        in_specs=[pl.BlockSpec((TM, D), lambda i, j: (i, 0)),
                  g.mod_spec(layer, 0, TM), g.mod_spec(layer, 1, TM),
                  pl.BlockSpec((None, D, tn), lambda i, j: (layer, 0, j))],
        out_specs=pl.BlockSpec((TM, tn), lambda i, j: (i, j)),
        scratch_shapes=[pltpu.VMEM((TM, D), BF16)],
        compiler_params=_cparams(("parallel", "arbitrary")), name=name,
    )(x, modflat, modflat, w)


def _inproj_conv_kernel(prev_ref, main_ref, next_ref, sh_ref, sc_ref, w_ref, cw_ref, o_ref, u_ref, *, n_lat_tiles,
                        tiles_per_batch, ctx_len, tc):
    i = pl.program_id(0)

    @pl.when(pl.program_id(1) == 0)
    def _():
        mod = lambda a: (_ln(a) * (1.0 + sc_ref[...]) + sh_ref[...]).astype(BF16)
        u_ref[:GRID_W, :] = mod(prev_ref[...])
        u_ref[GRID_W:GRID_W + TM, :] = mod(main_ref[...])
        u_ref[GRID_W + TM:, :] = mod(next_ref[...])

    is_ctx = i >= n_lat_tiles
    lat = jnp.logical_not(is_ctx)
    ib = i % tiles_per_batch
    top_ok = jnp.logical_and(ib != 0, lat)
    bot_ok = jnp.logical_and(ib != tiles_per_batch - 1, lat)
    t = lax.broadcasted_iota(jnp.int32, (TM, 1), 0)
    up_ok = jnp.logical_and(lat, jnp.logical_or(t >= GRID_W, top_ok))
    dn_ok = jnp.logical_and(lat, jnp.logical_or(t < TM - GRID_W, bot_ok))
    col = jnp.where(is_ctx, t & (ctx_len - 1), t & (GRID_W - 1))
    left_ok = col != 0
    right_ok = col != jnp.where(is_ctx, ctx_len - 1, GRID_W - 1)
    u = u_ref[...]
    step = 2 * LANE
    z_all = jnp.dot(u, w_ref[...], preferred_element_type=F32)
    for c0 in range(0, tc, step):
        cs = slice(c0, min(c0 + step, tc))
        z = z_all[:, cs]
        w = cw_ref[:, cs]
        up = jnp.where(up_ok, z[:TM], 0.0)
        main = z[GRID_W:GRID_W + TM]
        dn = jnp.where(dn_ok, z[2 * GRID_W:], 0.0)
        a0 = w[0:1] * up + w[3:4] * main + w[6:7] * dn
        a1 = w[1:2] * up + w[4:5] * main + w[7:8] * dn
        a2 = w[2:3] * up + w[5:6] * main + w[8:9] * dn
        o_ref[:, cs] = (a1 + jnp.where(left_ok, pltpu.roll(a0, 1, 0), 0.0)
                        + jnp.where(right_ok, pltpu.roll(a2, TM - 1, 0), 0.0))


def _inproj_conv_call(g, x, modflat, layer, w, conv_w):
    tc = 1664
    hb = TM // GRID_W
    nb64 = g.rows // GRID_W
    kern = functools.partial(_inproj_conv_kernel, n_lat_tiles=g.n_lat_rows // TM, tiles_per_batch=g.seq // TM,
                             ctx_len=g.ctx_len, tc=tc)
    return pl.pallas_call(
        kern, out_shape=jax.ShapeDtypeStruct((g.rows, ZC_W), F32), grid=(g.rows // TM, ZC_W // tc),
        in_specs=[pl.BlockSpec((GRID_W, D), lambda i, j: (jnp.maximum(i * hb - 1, 0), 0)),
                  pl.BlockSpec((TM, D), lambda i, j: (i, 0)),
                  pl.BlockSpec((GRID_W, D), lambda i, j: (jnp.minimum(i * hb + hb, nb64 - 1), 0)),
                  g.mod_spec(layer, 0, TM), g.mod_spec(layer, 1, TM),
                  pl.BlockSpec((None, D, tc), lambda i, j: (layer, 0, j)),
                  pl.BlockSpec((None, 9, tc), lambda i, j: (layer, 0, j))],
        out_specs=pl.BlockSpec((TM, tc), lambda i, j: (i, j)),
        scratch_shapes=[pltpu.VMEM((TM + 2 * GRID_W, D), BF16)],
        compiler_params=_cparams(("parallel", "arbitrary")), name="in_proj_conv",
    )(x, x, x, modflat, modflat, w, conv_w)


def _mlstm_kernel(q_ref, k_ref, v_ref, g_ref, gb_ref, inc_ref, eye_ref, y_ref, c_ref, n_ref, m_ref):
    d = pl.program_id(1)
    j = pl.program_id(2)

    @pl.when(j == 0)
    def _():
        c_ref[...] = jnp.zeros_like(c_ref)
        n_ref[...] = jnp.zeros_like(n_ref)
        m_ref[...] = jnp.zeros_like(m_ref)

    L = LCH
    inc = inc_ref[...]
    g = g_ref[...] + gb_ref[...]
    lf = -_softplus(-g)
    b_all = _sel_dot(inc, lf)
    r_all = g - pltpu.roll(b_all, LANE - A_H, 1)
    rt_all = sum(lax.dot_general(eye_ref[...], p, (((1,), (1,)), ((), ())), preferred_element_type=F32)
                 for p in _split3(r_all))
    is_f = d == 0
    valid = inc > 0
    lane = lax.broadcasted_iota(jnp.int32, (1, LANE), 1)
    m_all = m_ref[...]
    hs = range(A_H)
    li = [jnp.where(is_f, g[:, h:h + 1], g[:, 8 + h:9 + h]) for h in hs]
    b = [jnp.where(is_f, b_all[:, 4 + h:5 + h], b_all[:, 12 + h:13 + h]) for h in hs]
    r_row = [jnp.where(is_f, rt_all[h:h + 1, :], rt_all[8 + h:9 + h, :]) for h in hs]
    b_last = [jnp.where(is_f, b_all[L - 1:L, 4 + h:5 + h], b_all[0:1, 12 + h:13 + h]) for h in hs]
    m_prev = [m_all[:, h:h + 1] for h in hs]
    qh = [q_ref[:, h * A_HP:(h + 1) * A_HP] for h in hs]
    kh = [k_ref[:, h * A_HP:(h + 1) * A_HP] * (A_HD ** -0.5) for h in hs]
    vh = [v_ref[:, h * A_HP:(h + 1) * A_HP] for h in hs]
    qk = [_dot_nt(qh[h], kh[h]) for h in hs]
    qc = [_bdot(qh[h], c_ref[h]) for h in hs]
    log_w = [b_last[h] - b[h] + li[h] for h in hs]
    m_new = [jnp.maximum(b_last[h] + m_prev[h], jnp.max(log_w[h], axis=0, keepdims=True)) for h in hs]
    kw = [kh[h] * jnp.exp(log_w[h] - m_new[h]) for h in hs]
    kv = [_dot_tn(kw[h], vh[h]) for h in hs]
    log_d = [jnp.where(valid, b[h] + r_row[h], -jnp.inf) for h in hs]
    m_inter = [b[h] + m_prev[h] for h in hs]
    m_t = [jnp.maximum(m_inter[h], jnp.max(log_d[h], axis=-1, keepdims=True)) for h in hs]
    s = [qk[h] * jnp.exp(log_d[h] - m_t[h]) for h in hs]
    sv = [_bdot(s[h], vh[h]) for h in hs]
    outs = []
    for h in hs:
        w_inter = jnp.exp(m_inter[h] - m_t[h])
        num = sv[h] + w_inter * qc[h]
        den = jnp.sum(s[h], axis=-1, keepdims=True) + w_inter * jnp.sum(qh[h] * n_ref[h], axis=-1, keepdims=True)
        outs.append(num / jnp.maximum(jnp.abs(den), jnp.exp(-m_t[h])))
        cd = jnp.exp(b_last[h] + m_prev[h] - m_new[h])
        c_ref[h] = cd * c_ref[h] + kv[h]
        n_ref[h] = cd * n_ref[h] + jnp.sum(kw[h], axis=0, keepdims=True)
        m_all = jnp.where(lane == h, m_new[h], m_all)
    m_ref[...] = m_all
    y_ref[...] = jnp.concatenate(outs, axis=-1)


def _mlstm_call(g, zc, zr, gate_b, layer, inc):
    blk = g.chunk_block
    qkv = lambda c: pl.BlockSpec((LCH, A_W), lambda b, d, j: (blk(b, d, j), c))
    return pl.pallas_call(
        _mlstm_kernel, out_shape=jax.ShapeDtypeStruct((2, g.rows, A_W), F32),
        grid=(g.batch, 2, g.chunk_steps),
        in_specs=[qkv(ZC_AQ // A_W), qkv(ZC_AK // A_W), qkv(ZC_AV // A_W),
                  pl.BlockSpec((LCH, LANE), lambda b, d, j: (blk(b, d, j), ZR_AG // LANE)),
                  pl.BlockSpec((None, 1, LANE), lambda b, d, j: (layer, 0, 0)),
                  pl.BlockSpec((None, LCH, LCH), lambda b, d, j: (d, 0, 0)),
                  pl.BlockSpec((LANE, LANE), lambda b, d, j: (0, 0))],
        out_specs=pl.BlockSpec((None, LCH, A_W), lambda b, d, j: (d, blk(b, d, j), 0)),
        scratch_shapes=[pltpu.VMEM((A_H, A_HP, A_HP), F32), pltpu.VMEM((A_H, 1, A_HP), F32),
                        pltpu.VMEM((1, LANE), F32)],
        compiler_params=_cparams(("arbitrary", "arbitrary", "arbitrary")), name="mlstm_scan",
    )(zc, zc, zc, zr, gate_b, inc, jnp.eye(LANE, dtype=BF16))


def _gla_kernel(q_ref, k_ref, v_ref, code_ref, gup_ref, gb_ref, w_ref, mask_ref, qm_ref, vm_ref, bd_ref,
                y_ref, st_ref, *, nl):
    j = pl.program_id(2)

    @pl.when(j == 0)
    def _():
        st_ref[...] = jnp.zeros_like(st_ref)

    L = LCH
    x = _bdot(code_ref[...], gup_ref[...]) + gb_ref[...]
    glog = -_softplus(-x) * (1.0 / C_TAU)
    e = _sel_dot(w_ref[...], glog)
    q = q_ref[...] * (C_DK ** -0.5)
    k = k_ref[...]
    v = v_ref[...]
    stack = lambda a: jnp.concatenate([a * qm_ref[h] for h in range(C_H)], axis=0)
    scale = [jnp.exp(e[lv * L:(lv + 1) * L]) for lv in range(nl)]
    prods = [_dot_nt(stack(q * s_), k * s_) for s_ in scale] + [_dot_nt(stack(q), k)]
    att = sum(mask_ref[lv * C_H * L:(lv + 1) * C_H * L, :] * prods[lv] for lv in range(nl + 1))
    o = _dot_nt(q * jnp.exp(e[nl * L:(nl + 1) * L]), st_ref[...])
    av = [_bdot(att[h * L:(h + 1) * L], v) for h in range(C_H)]
    for h in range(C_H):
        o = o + vm_ref[h] * av[h]
    y_ref[...] = o
    ke = k * jnp.exp(e[(nl + 1) * L:(nl + 2) * L])
    tot = e[(nl + 2) * L:(nl + 2) * L + 1]
    st_ref[...] = st_ref[...] * jnp.exp(tot) + bd_ref[...] * _dot_tn(v, ke)


def _gla_call(g, zc, zr, gate_up, gate_b, layer, consts):
    wmat, masks, qmask, vmask, bd, nl = consts
    blk = g.chunk_block
    const = lambda a: pl.BlockSpec(a.shape, lambda b, d, j: (0,) * a.ndim)
    return pl.pallas_call(
        functools.partial(_gla_kernel, nl=nl), out_shape=jax.ShapeDtypeStruct((2, g.rows, C_VW), F32),
        grid=(g.batch, 2, g.chunk_steps),
        in_specs=[pl.BlockSpec((LCH, C_KP), lambda b, d, j: (blk(b, d, j), ZC_CQ // C_KP)),
                  pl.BlockSpec((LCH, C_KP), lambda b, d, j: (blk(b, d, j), ZC_CK // C_KP)),
                  pl.BlockSpec((LCH, C_VW), lambda b, d, j: (blk(b, d, j), ZC_CV // C_VW)),
                  pl.BlockSpec((LCH, LANE), lambda b, d, j: (blk(b, d, j), ZR_CG // LANE)),
                  pl.BlockSpec((None, None, LANE, C_KP), lambda b, d, j: (layer, d, 0, 0)),
                  pl.BlockSpec((None, None, 1, C_KP), lambda b, d, j: (layer, d, 0, 0)),
                  pl.BlockSpec((None,) + wmat.shape[1:], lambda b, d, j: (d, 0, 0)),
                  pl.BlockSpec((None,) + masks.shape[1:], lambda b, d, j: (d, 0, 0)),
                  const(qmask), const(vmask), const(bd)],
        out_specs=pl.BlockSpec((None, LCH, C_VW), lambda b, d, j: (d, blk(b, d, j), 0)),
        scratch_shapes=[pltpu.VMEM((C_VW, C_KP), F32)],
        compiler_params=_cparams(("arbitrary", "arbitrary", "arbitrary")), name="gla_scan",
    )(zc, zc, zc, zr, gate_up, gate_b, wmat, masks, qmask, vmask, bd)


def _rwkv_kernel(r_ref, k_ref, v_ref, wc_ref, ac_ref, w0_ref, wup_ref, a0_ref, aup_ref, kk_ref, ka_ref, rk_ref,
                 incbd_ref, totbd_ref, sm_ref, im_ref, seg_ref, eye_ref, y_ref, bonus_ref,
                 st_ref, lhs_ref, bes_ref, pt_ref, y1_ref, gam_ref):
    d = pl.program_id(1)
    j = pl.program_id(2)

    @pl.when(j == 0)
    def _():
        st_ref[...] = jnp.zeros_like(st_ref)

    r = r_ref[...]
    k = k_ref[...]
    v = v_ref[...]
    seg128 = seg_ref[:LANE, :LANE]

    def seg(a_):
        return jnp.concatenate([_dot_sel(a_[:, t_ * LANE:(t_ + 1) * LANE], seg128) for t_ in range(B_W // LANE)],
                               axis=-1)

    kk = k * kk_ref[...]
    kk = kk / jnp.maximum(jnp.sqrt(seg(kk * kk)), 1e-12)
    w_log = -_softplus(-(w0_ref[...] + _bdot(jnp.tanh(wc_ref[...]), wup_ref[...]))) - 0.5
    lw = -jnp.exp(w_log)
    a = _sigmoid(a0_ref[...] + _bdot(ac_ref[...], aup_ref[...]))
    kd = k * (1.0 + (a - 1.0) * ka_ref[...])
    be = kk * a
    bonus_ref[...] = seg(r * kd * rk_ref[...]) * v

    cs = _sel_dot(incbd_ref[...], lw)
    tot = _sel_dot(totbd_ref[...], lw)
    e_neg = jnp.exp(-cs)
    e_end = jnp.exp(tot - cs)
    al_s = kk * jnp.exp(cs - lw)
    r_s = r * jnp.exp(cs)
    k_s = kd * e_neg
    b_s = be * e_neg
    ke_s = kd * e_end
    be_s = be * e_end

    L, c, nsub, npair, hd = LCH, RC, LCH // RC, B_H // 2, B_HD
    sm = sm_ref[...]
    im = im_ref[...]
    eye = eye_ref[...]
    lane = lax.broadcasted_iota(jnp.int32, (1, LANE), 1)
    hmask = [(lane < hd).astype(F32), (lane >= hd).astype(F32)]
    for i in range(nsub):
        gam_ref[i] = jnp.exp(tot[i * c:i * c + 1])

    heads = [(p, e) for p in range(npair) for e in range(2)]
    lanes = lambda a, p: a[:, p * LANE:(p + 1) * LANE]
    gg = [_dot_nt(jnp.concatenate([lanes(al_s, p), lanes(r_s, p)], axis=0) * hmask[e],
                  jnp.concatenate([lanes(k_s, p), lanes(b_s, p)], axis=0)) for p, e in heads]
    a_k = [g_[:L, :L] * sm for g_ in gg]
    n_m = [g_[:L, L:] * sm for g_ in gg]
    b_k = [g_[L:, :L] * im for g_ in gg]
    d_m = [g_[L:, L:] * im for g_ in gg]
    tinv = [eye - n_ for n_ in n_m]
    npow = n_m
    for _ in range(int(math.log2(c)) - 1):
        npow = [_bdot(n_, n_) for n_ in npow]
        tinv = [t_ + _bdot(t_, n_) for t_, n_ in zip(tinv, npow)]
    ve = [lanes(v, p) * hmask[e] for p, e in heads]
    at = [_bdot(t_, lanes(al_s, p) * hmask[e]) for t_, (p, e) in zip(tinv, heads)]
    av = [_bdot(a_, v_) for a_, v_ in zip(a_k, ve)]
    w1 = [_bdot(t_, x_) for t_, x_ in zip(tinv, av)]
    rt = [lanes(r_s, p) * hmask[e] - _bdot(d_, a_) for d_, a_, (p, e) in zip(d_m, at, heads)]
    bv = [_bdot(b_, v_) for b_, v_ in zip(b_k, ve)]
    y1 = [x_ - _bdot(d_, w_) for x_, d_, w_ in zip(bv, d_m, w1)]
    for p in range(npair):
        h0, h1 = 2 * p, 2 * p + 1
        vp, kep, bep = lanes(v, p), lanes(ke_s, p), lanes(be_s, p)
        for i in range(nsub):
            rw = slice(i * c, (i + 1) * c)
            lhs_ref[i, p] = jnp.concatenate([at[h0][rw], at[h1][rw], rt[h0][rw], rt[h1][rw]], axis=0).astype(BF16)
            bes = jnp.concatenate([bep[rw] * hmask[0], bep[rw] * hmask[1]], axis=0)
            bes_ref[i, p] = bes.astype(BF16)
            y1_ref[i, p] = jnp.concatenate([y1[h0][rw, :hd], y1[h1][rw, hd:]], axis=0)
            pt_ref[i, p] = _dot_tn(
                jnp.concatenate([vp[rw, :hd], vp[rw, hd:], w1[h0][rw, :hd], w1[h1][rw, hd:]], axis=0),
                jnp.concatenate([kep[rw] * hmask[0], kep[rw] * hmask[1], -bes], axis=0))

    for ii in range(nsub):
        i = jnp.where(d == 0, ii, nsub - 1 - ii)
        gam = gam_ref[i]
        stp = [st_ref[p] for p in range(npair)]
        x = [lax.dot_general(lhs_ref[i, p], stp[p].astype(BF16), (((1,), (1,)), ((), ())),
                             preferred_element_type=F32) for p in range(npair)]
        for p in range(npair):
            st_ref[p] = (stp[p] * gam[:, p * LANE:(p + 1) * LANE] + pt_ref[i, p]
                         - lax.dot_general(x[p][:2 * c].astype(BF16), bes_ref[i, p], (((0,), (0,)), ((), ())),
                                           preferred_element_type=F32))
        ys = []
        for p in range(npair):
            yp = x[p][2 * c:] + y1_ref[i, p]
            ys += [yp[:c], yp[c:]]
        y_ref[pl.ds(pl.multiple_of(i * c, c), c), :] = jnp.concatenate(ys, axis=-1)


def _rwkv_call(g, zc, zr, p, layer, rc):
    blk = g.chunk_block
    nsub, npair = LCH // RC, B_H // 2
    rkv = lambda c: pl.BlockSpec((LCH, B_W), lambda b, d, j: (blk(b, d, j), c))
    code = lambda c: pl.BlockSpec((LCH, LANE), lambda b, d, j: (blk(b, d, j), c))
    per_dir_vec = pl.BlockSpec((None, None, 1, B_W), lambda b, d, j: (layer, d, 0, 0))
    per_dir_mat = pl.BlockSpec((None, None, LANE, B_W), lambda b, d, j: (layer, d, 0, 0))
    vec = pl.BlockSpec((None, 1, B_W), lambda b, d, j: (layer, 0, 0))
    dirc = lambda a: pl.BlockSpec((None,) + a.shape[1:], lambda b, d, j: (d,) + (0,) * (a.ndim - 1))
    const = lambda a: pl.BlockSpec(a.shape, lambda b, d, j: (0,) * a.ndim)
    out = pl.BlockSpec((None, LCH, B_W), lambda b, d, j: (d, blk(b, d, j), 0))
    return pl.pallas_call(
        _rwkv_kernel,
        out_shape=(jax.ShapeDtypeStruct((2, g.rows, B_W), F32), jax.ShapeDtypeStruct((2, g.rows, B_W), F32)),
        grid=(g.batch, 2, g.chunk_steps),
        in_specs=[rkv(ZC_BR // B_W), rkv(ZC_BK // B_W), rkv(ZC_BV // B_W),
                  code(ZR_BW // LANE), code(ZR_BA // LANE),
                  per_dir_vec, per_dir_mat, per_dir_vec, per_dir_mat, vec, vec, vec,
                  dirc(rc["incbd"]), const(rc["totbd"]), dirc(rc["sm"]), dirc(rc["im"]), const(rc["seg"]),
                  const(rc["eye"])],
        out_specs=(out, out),
        scratch_shapes=[pltpu.VMEM((npair, B_HD, LANE), F32), pltpu.VMEM((nsub, npair, 4 * RC, LANE), BF16),
                        pltpu.VMEM((nsub, npair, 2 * RC, LANE), BF16), pltpu.VMEM((nsub, npair, B_HD, LANE), F32),
                        pltpu.VMEM((nsub, npair, 2 * RC, B_HD), F32), pltpu.VMEM((nsub, 1, B_W), F32)],
        compiler_params=_cparams(("arbitrary", "arbitrary", "arbitrary")), name="rwkv_scan",
    )(zc, zc, zc, zr, zr, p["w0"], p["w_up"], p["a0"], p["a_up"], p["kk"], p["ka"], p["rk"],
      rc["incbd"], rc["totbd"], rc["sm"], rc["im"], rc["seg"], rc["eye"])


def _post_kernel(ya_ref, yb_ref, bn_ref, yc_ref, ao_ref, bg_ref, co_ref, gna_ref, gnb_ref, gup_ref, gnc_ref,
                 sega_ref, segb_ref, segc_ref, o_ref):
    def seg_sum(x, seg):
        w = seg.shape[0]
        return jnp.concatenate([_dot_sel(x[:, i * w:(i + 1) * w], seg) for i in range(x.shape[1] // w)], axis=-1)

    ha = ya_ref[0] + ya_ref[1]
    sega = sega_ref[...]
    hf = ha - seg_sum(ha, sega) * (1.0 / A_HD)
    hn = hf * lax.rsqrt(seg_sum(hf * hf, sega) * (1.0 / A_HD) + HEAD_EPS)
    h_a = _sigmoid(ao_ref[...]) * (hn * gna_ref[...])

    yb = yb_ref[0] + yb_ref[1]
    segb = segb_ref[...]
    yf = yb - seg_sum(yb, segb) * (1.0 / B_HD)
    yn = yf * lax.rsqrt(seg_sum(yf * yf, segb) * (1.0 / B_HD) + B_GN_EPS)
    ob = yn * gnb_ref[0] + gnb_ref[1] + bn_ref[0] + bn_ref[1]
    h_b = ob * _bdot(_sigmoid(bg_ref[...]), gup_ref[...])

    oc = yc_ref[0] + yc_ref[1]
    on = oc * lax.rsqrt(_dot_sel(oc * oc, segc_ref[...]) * (1.0 / C_DV) + HEAD_EPS) * gnc_ref[...]
    co = co_ref[...]
    h_c = on * (co * _sigmoid(co))
    o_ref[...] = jnp.concatenate([h_a, h_b, h_c], axis=-1).astype(BF16)


def _post_call(g, ya, yb, bonus, yc, zr, p, layer, segs):
    tm = TM_S
    two = lambda w: pl.BlockSpec((2, tm, w), lambda i: (0, i, 0))
    zcol = lambda w, off: pl.BlockSpec((tm, w), lambda i: (i, off // w))
    const = lambda a: pl.BlockSpec(a.shape, lambda i: (0,) * a.ndim)
    return pl.pallas_call(
        _post_kernel, out_shape=jax.ShapeDtypeStruct((g.rows, H_W), BF16), grid=(g.rows // tm,),
        in_specs=[two(A_W), two(B_W), two(B_W), two(C_VW),
                  zcol(A_W, ZR_AO), zcol(LANE, ZR_BG), zcol(C_VW, ZR_CO),
                  pl.BlockSpec((None, 1, A_W), lambda i: (layer, 0, 0)),
                  pl.BlockSpec((None, 2, 1, B_W), lambda i: (layer, 0, 0, 0)),
                  pl.BlockSpec((None, LANE, B_W), lambda i: (layer, 0, 0)),
                  pl.BlockSpec((None, 1, C_VW), lambda i: (layer, 0, 0)),
                  const(segs[0]), const(segs[1]), const(segs[2])],
        out_specs=pl.BlockSpec((tm, H_W), lambda i: (i, 0)),
        compiler_params=_cparams(("parallel",)), name="mixer_post",
    )(ya, yb, bonus, yc, zr, zr, zr, p["gn_a"], p["gn_b"], p["g_up"], p["gn_c"], *segs)


def _merge_kernel(h_ref, mg_ref, x_ref, gt_ref, lng_ref, lnb_ref, wcat_ref, wout_ref, o_ref, *, alpha):
    h = h_ref[...]
    mg = mg_ref[...]
    o1, o2 = A_W, A_W + B_W
    merged = (_sigmoid(mg[:, :D]) * jnp.dot(h[:, :o1], wcat_ref[:o1], preferred_element_type=F32)
              + _sigmoid(mg[:, D:2 * D]) * jnp.dot(h[:, o1:o2], wcat_ref[o1:o2], preferred_element_type=F32)
              + _sigmoid(mg[:, 2 * D:]) * jnp.dot(h[:, o2:], wcat_ref[o2:], preferred_element_type=F32))
    y = jnp.dot(merged.astype(BF16), wout_ref[...], preferred_element_type=F32)
    o_ref[...] = _ln(alpha * x_ref[...] + gt_ref[...] * y) * lng_ref[...] + lnb_ref[...]


def _ln_spec(layer, sub, which):
    return pl.BlockSpec((None, None, None, 1, D), lambda i, *_: (layer, sub, which, 0, 0))


def _merge_call(g, h, zr, x, modflat, ln_post, wcat, wout, layer, alpha):
    tm = TM_S
    return pl.pallas_call(
        functools.partial(_merge_kernel, alpha=alpha), out_shape=jax.ShapeDtypeStruct((g.rows, D), F32),
        grid=(g.rows // tm,),
        in_specs=[pl.BlockSpec((tm, H_W), lambda i: (i, 0)),
                  pl.BlockSpec((tm, 3 * D), lambda i: (i, 0)),
                  pl.BlockSpec((tm, D), lambda i: (i, 0)),
                  g.mod_spec(layer, 2, tm), _ln_spec(layer, 0, 0), _ln_spec(layer, 0, 1),
                  pl.BlockSpec((None, H_W, D), lambda i: (layer, 0, 0), pipeline_mode=pl.Buffered(1)),
                  pl.BlockSpec((None, D, D), lambda i: (layer, 0, 0), pipeline_mode=pl.Buffered(1))],
        out_specs=pl.BlockSpec((tm, D), lambda i: (i, 0)),
        compiler_params=_cparams(("parallel",)), name="branch_merge",
    )(h, zr, x, modflat, ln_post, ln_post, wcat, wout)


def _ffn_kernel(x_ref, sh_ref, sc_ref, gt_ref, lng_ref, lnb_ref, wg_ref, wu_ref, wd_ref, o_ref, u_ref, acc_ref,
                *, alpha):
    j = pl.program_id(1)

    @pl.when(j == 0)
    def _():
        u_ref[...] = (_ln(x_ref[...]) * (1.0 + sc_ref[...]) + sh_ref[...]).astype(BF16)
        acc_ref[...] = jnp.zeros_like(acc_ref)

    acc_ref[...] += _swiglu_tile(u_ref[...], wg_ref[0], wu_ref[0], wd_ref[0], j)

    @pl.when(j == pl.num_programs(1) - 1)
    def _():
        o_ref[...] = _ln(alpha * x_ref[...] + gt_ref[...] * acc_ref[...]) * lng_ref[...] + lnb_ref[...]


def _ff_start(j):
    return pl.multiple_of(jnp.minimum(j * TF, D_FF - TF), LANE)


def _swiglu_tile(u, wg, wu, wd, j):
    gate = jnp.dot(u, wg, preferred_element_type=F32)
    up = jnp.dot(u, wu, preferred_element_type=F32)
    col = _ff_start(j) + lax.broadcasted_iota(jnp.int32, (1, TF), 1)
    hid = jnp.where(col >= j * TF, (gate * _sigmoid(gate)) * up, 0.0)
    return jnp.dot(hid.astype(BF16), wd, preferred_element_type=F32)


def _ffn_call(g, x, modflat, ln_post, wg, wu, wd, layer, idx, alpha, out_rows):
    el = pl.Element
    return pl.pallas_call(
        functools.partial(_ffn_kernel, alpha=alpha), out_shape=jax.ShapeDtypeStruct((out_rows, D), F32),
        grid=(out_rows // TM, N_FF_TILES),
        in_specs=[pl.BlockSpec((TM, D), lambda i, j: (i, 0)),
                  g.mod_spec(layer, 3, TM), g.mod_spec(layer, 4, TM), g.mod_spec(layer, 5, TM),
                  _ln_spec(layer, 1, 0), _ln_spec(layer, 1, 1),
                  pl.BlockSpec((el(1), el(D), el(TF)), lambda i, j: (idx, 0, _ff_start(j))),
                  pl.BlockSpec((el(1), el(D), el(TF)), lambda i, j: (idx, 0, _ff_start(j))),
                  pl.BlockSpec((el(1), el(TF), el(D)), lambda i, j: (idx, _ff_start(j), 0))],
        out_specs=pl.BlockSpec((TM, D), lambda i, j: (i, 0)),
        scratch_shapes=[pltpu.VMEM((TM, D), BF16), pltpu.VMEM((TM, D), F32)],
        compiler_params=_cparams(("parallel", "arbitrary")), name="ffn_swiglu",
    )(x, modflat, modflat, modflat, ln_post, ln_post, wg, wu, wd)


def _router_kernel(x_ref, sh_ref, sc_ref, w_ref, b_ref, u_ref, sel_ref):
    u = _ln(x_ref[...]) * (1.0 + sc_ref[...]) + sh_ref[...]
    u_ref[...] = u
    lane = lax.broadcasted_iota(jnp.int32, (1, LANE), 1)
    logits = jnp.where(lane < N_EXPERTS, _dot3(u, w_ref[...]) + b_ref[...], -jnp.inf)
    m1 = jnp.max(logits, axis=-1, keepdims=True)
    i1 = jnp.min(jnp.where(logits == m1, lane, LANE), axis=-1, keepdims=True)
    rest = jnp.where(lane == i1, -jnp.inf, logits)
    m2 = jnp.max(rest, axis=-1, keepdims=True)
    i2 = jnp.min(jnp.where(rest == m2, lane, LANE), axis=-1, keepdims=True)
    e2 = jnp.exp(m2 - m1)
    p1 = 1.0 / (1.0 + e2)
    sel_ref[...] = (jnp.where(lane == 0, i1.astype(F32), 0.0) + jnp.where(lane == 1, i2.astype(F32), 0.0)
                    + jnp.where(lane == 2, p1, 0.0) + jnp.where(lane == 3, e2 * p1, 0.0))


def _router_call(g, x, modflat, router, router_b, layer, idx):
    return pl.pallas_call(
        _router_kernel,
        out_shape=(jax.ShapeDtypeStruct((g.rows, D), F32), jax.ShapeDtypeStruct((g.rows, LANE), F32)),
        grid=(g.rows // TM,),
        in_specs=[pl.BlockSpec((TM, D), lambda i: (i, 0)), g.mod_spec(layer, 3, TM), g.mod_spec(layer, 4, TM),
                  pl.BlockSpec((None, D, LANE), lambda i: (idx, 0, 0)),
                  pl.BlockSpec((None, 1, LANE), lambda i: (idx, 0, 0))],
        out_specs=(pl.BlockSpec((TM, D), lambda i: (i, 0)), pl.BlockSpec((TM, LANE), lambda i: (i, 0))),
        compiler_params=_cparams(("parallel",)), name="moe_router",
    )(x, modflat, modflat, router, router_b)


def _route(sel, n_slots):
    t = sel.shape[0]
    expert = jnp.concatenate([sel[:, 0], sel[:, 1]]).astype(jnp.int32)
    prob = jnp.concatenate([sel[:, 2], sel[:, 3]])
    onehot = (expert[:, None] == jnp.arange(N_EXPERTS, dtype=jnp.int32)[None, :]).astype(jnp.int32)
    csum = jnp.cumsum(onehot, axis=0)
    rank = jnp.sum(csum * onehot, axis=1) - 1
    count = csum[-1]
    padded = ((count + TM - 1) // TM) * TM
    group_end = jnp.cumsum(padded)
    pos = jnp.sum(onehot * (group_end - padded)[None, :], axis=1) + rank
    token = jnp.concatenate([jnp.arange(t, dtype=F32)] * 2)
    slots = jnp.zeros((n_slots, 2), F32).at[pos].set(jnp.stack([token, prob], axis=1))
    src, prob_sorted = slots[:, 0].astype(jnp.int32), slots[:, 1]
    tile_start = jnp.arange(n_slots // TM, dtype=jnp.int32) * TM
    tile_expert = jnp.sum((tile_start[:, None] >= group_end[None, :]).astype(jnp.int32), axis=1)
    tile_valid = (tile_expert < N_EXPERTS).astype(jnp.int32)
    return src, prob_sorted, jnp.minimum(tile_expert, N_EXPERTS - 1), tile_valid, pos


def _row_gather_kernel(idx_ref, src_ref, o_ref, sem, *, n):
    def row_copy(r):
        return pltpu.make_async_copy(src_ref.at[pl.ds(idx_ref[0, 0, r], 1)], o_ref.at[pl.ds(r, 1)], sem)

    def start(r, carry):
        row_copy(2 * r).start(priority=0)
        row_copy(2 * r + 1).start(priority=1)
        return carry

    def wait(r, carry):
        row_copy(r).wait()
        return carry

    lax.fori_loop(0, n // 2, start, 0, unroll=4)
    lax.fori_loop(0, n, wait, 0, unroll=8)


def _row_gather_call(src, idx):
    n = idx.shape[0]
    return pl.pallas_call(
        functools.partial(_row_gather_kernel, n=TM), out_shape=jax.ShapeDtypeStruct((n, D), F32),
        grid=(n // TM,),
        in_specs=[pl.BlockSpec((1, 1, TM), lambda i: (i, 0, 0), memory_space=pltpu.SMEM),
                  pl.BlockSpec(memory_space=pl.ANY)],
        out_specs=pl.BlockSpec((TM, D), lambda i: (i, 0)),
        scratch_shapes=[pltpu.SemaphoreType.DMA(())],
        compiler_params=_cparams(("arbitrary",)), name="moe_gather",
    )(idx.reshape(n // TM, 1, TM), src)


def _moe_group_kernel(te_ref, valid_ref, xs_ref, pr_ref, wg_ref, wu_ref, wd_ref, o_ref, u_ref, acc_ref):
    i = pl.program_id(0)
    j = pl.program_id(1)

    @pl.when(j == 0)
    def _():
        u_ref[...] = xs_ref[...].astype(BF16)
        acc_ref[...] = jnp.zeros_like(acc_ref)

    @pl.when(valid_ref[i] > 0)
    def _():
        acc_ref[...] += _swiglu_tile(u_ref[...], wg_ref[0, 0], wu_ref[0, 0], wd_ref[0, 0], j)

    @pl.when(j == pl.num_programs(1) - 1)
    def _():
        o_ref[...] = pr_ref[...] * acc_ref[...]


def _moe_group_call(xs, prob_sorted, tile_expert, tile_valid, wg, wu, wd, idx):
    n = xs.shape[0]
    el = pl.Element
    fcol = lambda i, j, te, va: _ff_start(jnp.where(va[i] > 0, j, 0))
    grid_spec = pltpu.PrefetchScalarGridSpec(
        num_scalar_prefetch=2, grid=(n // TM, N_FF_TILES),
        in_specs=[pl.BlockSpec((TM, D), lambda i, j, te, va: (i, 0)),
                  pl.BlockSpec((TM, 1), lambda i, j, te, va: (i, 0)),
                  pl.BlockSpec((el(1), el(1), el(D), el(TF)),
                               lambda i, j, te, va: (idx, te[i], 0, fcol(i, j, te, va))),
                  pl.BlockSpec((el(1), el(1), el(D), el(TF)),
                               lambda i, j, te, va: (idx, te[i], 0, fcol(i, j, te, va))),
                  pl.BlockSpec((el(1), el(1), el(TF), el(D)),
                               lambda i, j, te, va: (idx, te[i], fcol(i, j, te, va), 0))],
        out_specs=pl.BlockSpec((TM, D), lambda i, j, te, va: (i, 0)),
        scratch_shapes=[pltpu.VMEM((TM, D), BF16), pltpu.VMEM((TM, D), F32)])
    return pl.pallas_call(
        _moe_group_kernel, out_shape=jax.ShapeDtypeStruct((n, D), F32), grid_spec=grid_spec,
        compiler_params=_cparams(("arbitrary", "arbitrary")), name="moe_experts",
    )(tile_expert, tile_valid, xs, prob_sorted.reshape(n, 1), wg, wu, wd)


def _moe_combine_kernel(pos_ref, ys_ref, x_ref, gt_ref, lng_ref, lnb_ref, o_ref, buf_ref, sem, *, alpha):
    def row_copy(r):
        return pltpu.make_async_copy(ys_ref.at[pl.ds(pos_ref[0, 0, r], 1)], buf_ref.at[pl.ds(r, 1)], sem)

    def start(r, carry):
        row_copy(2 * r).start(priority=0)
        row_copy(2 * r + 1).start(priority=1)
        return carry

    def wait(r, carry):
        row_copy(r).wait()
        return carry

    lax.fori_loop(0, TM, start, 0, unroll=4)
    lax.fori_loop(0, 2 * TM, wait, 0, unroll=8)
    f = buf_ref[:TM, :] + buf_ref[TM:, :]
    o_ref[...] = _ln(alpha * x_ref[...] + gt_ref[...] * f) * lng_ref[...] + lnb_ref[...]


def _moe_combine_call(g, ys, pos, x, modflat, ln_post, layer, alpha, out_rows):
    nt = g.rows // TM
    pos2 = jnp.concatenate([pos[:g.rows].reshape(nt, 1, TM), pos[g.rows:].reshape(nt, 1, TM)], axis=-1)
    return pl.pallas_call(
        functools.partial(_moe_combine_kernel, alpha=alpha), out_shape=jax.ShapeDtypeStruct((out_rows, D), F32),
        grid=(out_rows // TM,),
        in_specs=[pl.BlockSpec((1, 1, 2 * TM), lambda i: (i, 0, 0), memory_space=pltpu.SMEM),
                  pl.BlockSpec(memory_space=pl.ANY),
                  pl.BlockSpec((TM, D), lambda i: (i, 0)),
                  g.mod_spec(layer, 5, TM), _ln_spec(layer, 1, 0), _ln_spec(layer, 1, 1)],
        out_specs=pl.BlockSpec((TM, D), lambda i: (i, 0)),
        scratch_shapes=[pltpu.VMEM((2 * TM, D), F32), pltpu.SemaphoreType.DMA(())],
        compiler_params=_cparams(("arbitrary",)), name="moe_combine",
    )(pos2, ys, x, modflat, ln_post, ln_post)


def kernel(x, c, ctx, c_ctx, w_mod, b_mod, w_in, conv_k, mlstm_gate_b, mlstm_gn, rwkv_w0, rwkv_w_up, rwkv_a0,
           rwkv_a_up, rwkv_g_up, rwkv_kk, rwkv_ka, rwkv_rk, rwkv_gn, gla_gate_up, gla_gate_b, gla_gn, w_branch_a,
           w_branch_b, w_branch_c, w_out, ln_post, ffn_w_gate, ffn_w_up, ffn_w_down, moe_router, moe_router_b,
           moe_w_gate, moe_w_up, moe_w_down):
    batch, seq, _ = x.shape
    ctx_len = ctx.shape[1]
    depth = w_mod.shape[0]
    alpha = (2.0 * depth) ** 0.25
    g = _Geom(batch, seq, ctx_len)

    n_conv = 5504
    w_conv = _pack_conv_cols(w_in[..., :n_conv].astype(BF16))
    w_rest = _pack_rest_cols(w_in[..., n_conv:].astype(BF16))
    conv_w = _pack_conv_cols(conv_k.reshape(depth, 9, n_conv))
    gate_b = _pad_last(mlstm_gate_b.reshape(depth, 1, 16), LANE)

    def rows_at(a, per, n):
        out = jnp.zeros((depth, 2, n, a.shape[-1]), a.dtype)
        for d in range(2):
            out = out.at[:, d, d * per:(d + 1) * per].set(a[:, d])
        return out

    rw = dict(w0=rwkv_w0.reshape(depth, 2, 1, B_W), w_up=rows_at(rwkv_w_up, B_W_LORA, LANE).astype(BF16),
              a0=rwkv_a0.reshape(depth, 2, 1, B_W), a_up=rows_at(rwkv_a_up, B_A_LORA, LANE).astype(BF16),
              kk=rwkv_kk.reshape(depth, 1, B_W), ka=rwkv_ka.reshape(depth, 1, B_W),
              rk=rwkv_rk.reshape(depth, 1, B_W))
    gla_up = _pad_last(rows_at(gla_gate_up, C_G_LORA, LANE), C_KP).astype(BF16)
    gla_b = _pad_last(gla_gate_b.reshape(depth, 2, 1, C_KW), C_KP)
    post_p = dict(gn_a=_pad_heads(mlstm_gn, A_H, A_HD, A_HP).reshape(depth, 1, A_W),
                  gn_b=rwkv_gn.reshape(depth, 2, 1, B_W), g_up=rwkv_g_up.astype(BF16),
                  gn_c=gla_gn.reshape(depth, 1, C_VW))
    wa = _pad_heads(jnp.swapaxes(w_branch_a, 1, 2), A_H, A_HD, A_HP)
    wcat = jnp.concatenate([jnp.swapaxes(wa, 1, 2), w_branch_b, w_branch_c], axis=1).astype(BF16)
    wout = w_out.astype(BF16)
    lnp = ln_post.reshape(depth, 2, 2, 1, D)
    f_wg, f_wu, f_wd = ffn_w_gate.astype(BF16), ffn_w_up.astype(BF16), ffn_w_down.astype(BF16)
    m_wg, m_wu, m_wd = moe_w_gate.astype(BF16), moe_w_up.astype(BF16), moe_w_down.astype(BF16)
    router = _pad_last(moe_router, LANE)
    router_b = _pad_last(moe_router_b.reshape(-1, 1, N_EXPERTS), LANE)

    inc_a = _mlstm_consts()
    gla_c = _gla_consts()
    rwkv_c = _rwkv_consts()
    segs = _seg_consts()

    cvec = jnp.concatenate([c, c_ctx[None, :], jnp.zeros((8 - batch - 1, D), F32)], axis=0)
    modflat = _mod_call(cvec, w_mod, b_mod).reshape(depth * 8, 1, 6 * D)

    xs = jnp.concatenate([x.reshape(batch * seq, D), ctx.reshape(batch * ctx_len, D)], axis=0)
    for l in range(depth):
        zc = _inproj_conv_call(g, xs, modflat, l, w_conv, conv_w)
        zr = _inproj_call(g, xs, modflat, l, w_rest, 2816, "in_proj_rest")
        ya = _mlstm_call(g, zc, zr, gate_b, l, inc_a)
        yb, bonus = _rwkv_call(g, zc, zr, rw, l, rwkv_c)
        yc = _gla_call(g, zc, zr, gla_up, gla_b, l, gla_c)
        h = _post_call(g, ya, yb, bonus, yc, zr, post_p, l, segs)
        xs = _merge_call(g, h, zr, xs, modflat, lnp, wcat, wout, l, alpha)
        i = l // 2
        out_rows = g.rows if l < depth - 1 else g.n_lat_rows
        if l % 2 == 0:
            xs = _ffn_call(g, xs, modflat, lnp, f_wg, f_wu, f_wd, l, i, alpha, out_rows)
        else:
            u, sel = _router_call(g, xs, modflat, router, router_b, l, i)
            n_slots = -(-(2 * g.rows + N_EXPERTS * (TM - 1)) // TM) * TM
            src, prob_sorted, tile_expert, tile_valid, pos = _route(sel, n_slots)
            ys = _moe_group_call(_row_gather_call(u, src), prob_sorted, tile_expert, tile_valid, m_wg, m_wu, m_wd, i)
            xs = _moe_combine_call(g, ys, pos, xs, modflat, lnp, l, alpha, out_rows)
    return xs.reshape(batch, seq, D)
```
